```python
import math
import jax, jax.numpy as jnp
from jax import lax
import numpy as np

D_MODEL = 1024
BATCH = 4
SEQ = 8192
DEPTH = 4

N_MIXERS = 2
N_A = (DEPTH + 1) // 2
N_B = DEPTH // 2
NORM_EPS = 1e-6
D_RNN = D_MODEL
RG_HEADS = 8
RG_BW = D_RNN // RG_HEADS
RG_CONV_W = 4
RG_C = 8.0
RG_RAD_MIN = 0.9
RG_RAD_MAX = 0.999
D_S5 = D_MODEL
S5_GC = 16
S5_G = D_S5 // S5_GC
S5_P = 64
S5_DT_MIN = 0.001
S5_DT_MAX = 0.1
D_FF = 3 * D_MODEL
FFN_CONV_W = 3

kernel_name = "hybrid_rglru_s5_convffn_trunk"


def _rmsnorm(x, g):
    x32 = x.astype(jnp.float32)
    var = jnp.mean(x32 * x32, axis=-1, keepdims=True)
    return (x32 * lax.rsqrt(var + NORM_EPS) * g.astype(jnp.float32)).astype(x.dtype)


def _causal_dwconv(x, w, b):
    k_w = w.shape[0]
    s = x.shape[1]
    xp = jnp.pad(x, ((0, 0), (k_w - 1, 0), (0, 0)))
    out = b
    for k in range(k_w):
        out = out + xp[:, k:k + s, :] * w[k]
    return out


def _real_scan_combine(e1, e2):
    a1, b1 = e1
    a2, b2 = e2
    return a1 * a2, a2 * b1 + b2


def _complex_scan_combine(e1, e2):
    a1r, a1i, b1r, b1i = e1
    a2r, a2i, b2r, b2i = e2
    ar = a2r * a1r - a2i * a1i
    ai = a2r * a1i + a2i * a1r
    br = a2r * b1r - a2i * b1i + b2r
    bi = a2r * b1i + a2i * b1r + b2i
    return ar, ai, br, bi


def _rglru_mixer(h, w_in, conv_w, conv_b, w_a, b_a, w_x, b_x, lam, w_out):
    bsz, s, _ = h.shape
    xg = h @ w_in
    xr, gate = xg[..., :D_RNN], xg[..., D_RNN:]
    xr = _causal_dwconv(xr, conv_w, conv_b)
    xh = xr.reshape(bsz, s, RG_HEADS, RG_BW)
    r = jax.nn.sigmoid(jnp.einsum('bshi,hij->bshj', xh, w_a) + b_a).reshape(bsz, s, D_RNN)
    ig = jax.nn.sigmoid(jnp.einsum('bshi,hij->bshj', xh, w_x) + b_x).reshape(bsz, s, D_RNN)
    log_a = -RG_C * r.astype(jnp.float32) * jax.nn.softplus(-lam.astype(jnp.float32))
    a = jnp.exp(log_a)
    mult = jnp.sqrt(-jnp.expm1(2.0 * log_a))
    bterm = mult * (ig * xr).astype(jnp.float32)
    _, hs = lax.associative_scan(_real_scan_combine, (a, bterm), axis=1)
    y = hs.astype(h.dtype) * jax.nn.gelu(gate)
    return y @ w_out


def _s5_mixer(h, w_in, a_re, a_im, log_dt, b_re, b_im, c_re, c_im, d, w_glu, w_out):
    bsz, s, _ = h.shape
    u = h @ w_in
    ug = u.reshape(bsz, s, S5_G, S5_GC).astype(jnp.float32)
    ar = a_re.astype(jnp.float32)
    ai = a_im.astype(jnp.float32)
    dt = jnp.exp(log_dt.astype(jnp.float32))[:, None]
    mag = jnp.exp(ar * dt)
    abr = mag * jnp.cos(ai * dt)
    abi = mag * jnp.sin(ai * dt)
    ur, ui = abr - 1.0, abi
    den = ar * ar + ai * ai
    wr = (ur * ar + ui * ai) / den
    wi = (ui * ar - ur * ai) / den
    br32, bi32 = b_re.astype(jnp.float32), b_im.astype(jnp.float32)
    bbr = wr[..., None] * br32 - wi[..., None] * bi32
    bbi = wr[..., None] * bi32 + wi[..., None] * br32
    bu_r = jnp.einsum('bsgc,gpc->bsgp', ug, bbr)
    bu_i = jnp.einsum('bsgc,gpc->bsgp', ug, bbi)
    a_r = jnp.broadcast_to(abr, (1, s, S5_G, S5_P))
    a_i = jnp.broadcast_to(abi, (1, s, S5_G, S5_P))
    _, _, hr, hi = lax.associative_scan(_complex_scan_combine, (a_r, a_i, bu_r, bu_i), axis=1)
    y = (jnp.einsum('bsgp,gcp->bsgc', hr, c_re.astype(jnp.float32))
         - jnp.einsum('bsgp,gcp->bsgc', hi, c_im.astype(jnp.float32)))
    y = y.reshape(bsz, s, D_S5).astype(h.dtype) + d * u
    g = jax.nn.gelu(y)
    gl = g @ w_glu
    out = gl[..., :D_S5] * jax.nn.sigmoid(gl[..., D_S5:])
    return out @ w_out


def _conv_ffn(h, w_up, conv_w, conv_b, w_down):
    up = _causal_dwconv(h @ w_up, conv_w, conv_b)
    return (jax.nn.gelu(up[..., :D_FF]) * up[..., D_FF:]) @ w_down


def setup_inputs(seed: int = 0) -> dict:
    key = jax.random.key(seed)
    ks = jax.random.split(key, 32)
    f32 = jnp.float32
    nrm = lambda k, shp, sc: jax.random.normal(k, shp, f32) * sc
    x = jax.random.normal(ks[0], (BATCH, SEQ, D_MODEL), f32)
    norm_mix_g = 1.0 + nrm(ks[1], (DEPTH, D_MODEL), 0.02)
    norm_ffn_g = 1.0 + nrm(ks[2], (DEPTH, D_MODEL), 0.02)
    norm_final_g = 1.0 + nrm(ks[3], (D_MODEL,), 0.02)
    rg_w_in = nrm(ks[4], (N_A, D_MODEL, 2 * D_RNN), D_MODEL ** -0.5)
    rg_conv_w = nrm(ks[5], (N_A, RG_CONV_W, D_RNN), RG_CONV_W ** -0.5)
    rg_conv_b = nrm(ks[6], (N_A, D_RNN), 0.01)
    rg_w_a = nrm(ks[7], (N_A, RG_HEADS, RG_BW, RG_BW), RG_BW ** -0.5)
    rg_b_a = nrm(ks[8], (N_A, RG_HEADS, RG_BW), 0.01)
    rg_w_x = nrm(ks[9], (N_A, RG_HEADS, RG_BW, RG_BW), RG_BW ** -0.5)
    rg_b_x = nrm(ks[10], (N_A, RG_HEADS, RG_BW), 0.01)
    a0 = jnp.sqrt(jax.random.uniform(ks[11], (N_A, D_RNN), f32,
                                     RG_RAD_MIN ** 2, RG_RAD_MAX ** 2))
    rg_lambda = jnp.log(a0) - jnp.log1p(-a0)
    rg_w_out = nrm(ks[12], (N_A, D_RNN, D_MODEL), D_RNN ** -0.5)
    s5_w_in = nrm(ks[13], (N_B, D_MODEL, D_S5), D_MODEL ** -0.5)
    s5_a_re = -0.5 + nrm(ks[14], (N_B, S5_G, S5_P), 0.01)
    s5_a_im = (math.pi * jnp.arange(S5_P, dtype=f32))[None, None, :] + nrm(ks[15], (N_B, S5_G, S5_P), 0.01)
    s5_log_dt = jax.random.uniform(ks[16], (N_B, S5_G), f32,
                                   math.log(S5_DT_MIN), math.log(S5_DT_MAX))
    s5_b_re = nrm(ks[17], (N_B, S5_G, S5_P, S5_GC), (2 * S5_GC) ** -0.5)
    s5_b_im = nrm(ks[18], (N_B, S5_G, S5_P, S5_GC), (2 * S5_GC) ** -0.5)
    s5_c_re = nrm(ks[19], (N_B, S5_G, S5_GC, S5_P), (0.5 * S5_P) ** -0.5)
    s5_c_im = nrm(ks[20], (N_B, S5_G, S5_GC, S5_P), (0.5 * S5_P) ** -0.5)
    s5_d = nrm(ks[21], (N_B, D_S5), 1.0)
    s5_w_glu = nrm(ks[22], (N_B, D_S5, 2 * D_S5), D_S5 ** -0.5)
    s5_w_out = nrm(ks[23], (N_B, D_S5, D_MODEL), D_S5 ** -0.5)
    ffn_w_up = nrm(ks[24], (DEPTH, D_MODEL, 2 * D_FF), D_MODEL ** -0.5)
    ffn_conv_w = nrm(ks[25], (DEPTH, FFN_CONV_W, 2 * D_FF), FFN_CONV_W ** -0.5)
    ffn_conv_b = nrm(ks[26], (DEPTH, 2 * D_FF), 0.01)
    ffn_w_down = nrm(ks[27], (DEPTH, D_FF, D_MODEL), D_FF ** -0.5)
    return {"x": x, "norm_mix_g": norm_mix_g, "norm_ffn_g": norm_ffn_g, "norm_final_g": norm_final_g,
            "rg_w_in": rg_w_in, "rg_conv_w": rg_conv_w, "rg_conv_b": rg_conv_b,
            "rg_w_a": rg_w_a, "rg_b_a": rg_b_a, "rg_w_x": rg_w_x, "rg_b_x": rg_b_x,
            "rg_lambda": rg_lambda, "rg_w_out": rg_w_out,
            "s5_w_in": s5_w_in, "s5_a_re": s5_a_re, "s5_a_im": s5_a_im, "s5_log_dt": s5_log_dt,
            "s5_b_re": s5_b_re, "s5_b_im": s5_b_im, "s5_c_re": s5_c_re, "s5_c_im": s5_c_im,
            "s5_d": s5_d, "s5_w_glu": s5_w_glu, "s5_w_out": s5_w_out,
            "ffn_w_up": ffn_w_up, "ffn_conv_w": ffn_conv_w, "ffn_conv_b": ffn_conv_b,
            "ffn_w_down": ffn_w_down}


def reference(x, norm_mix_g, norm_ffn_g, norm_final_g,
              rg_w_in, rg_conv_w, rg_conv_b, rg_w_a, rg_b_a, rg_w_x, rg_b_x, rg_lambda, rg_w_out,
              s5_w_in, s5_a_re, s5_a_im, s5_log_dt, s5_b_re, s5_b_im, s5_c_re, s5_c_im,
              s5_d, s5_w_glu, s5_w_out,
              ffn_w_up, ffn_conv_w, ffn_conv_b, ffn_w_down):
    h = x
    for i in range(DEPTH):
        hn = _rmsnorm(h, norm_mix_g[i])
        j = i // N_MIXERS
        if i % N_MIXERS == 0:
            mix = _rglru_mixer(hn, rg_w_in[j], rg_conv_w[j], rg_conv_b[j], rg_w_a[j], rg_b_a[j],
                               rg_w_x[j], rg_b_x[j], rg_lambda[j], rg_w_out[j])
        else:
            mix = _s5_mixer(hn, s5_w_in[j], s5_a_re[j], s5_a_im[j], s5_log_dt[j], s5_b_re[j],
                            s5_b_im[j], s5_c_re[j], s5_c_im[j], s5_d[j], s5_w_glu[j], s5_w_out[j])
        h = h + mix.astype(h.dtype)
        hn = _rmsnorm(h, norm_ffn_g[i])
        h = h + _conv_ffn(hn, ffn_w_up[i], ffn_conv_w[i], ffn_conv_b[i], ffn_w_down[i]).astype(h.dtype)
    return _rmsnorm(h, norm_final_g)
```

```python
import functools

import jax
import jax.numpy as jnp
from jax import lax
from jax.experimental import pallas as pl
from jax.experimental.pallas import tpu as pltpu

NORM_EPS = 1e-6
RG_C = 8.0
SUBLANES = 8
LANES = 128
VMEM_LIMIT_BYTES = 56 * 1024 * 1024

F32 = jnp.float32
BF16 = jnp.bfloat16


def _rmsnorm(x, g):
    var = jnp.mean(x * x, axis=-1, keepdims=True)
    return x * lax.rsqrt(var + NORM_EPS) * g


def _dot(a, b):
    return jnp.dot(a, b, preferred_element_type=F32)


def _row_ids(shape):
    return lax.broadcasted_iota(jnp.int32, shape, 0)


def _ffn_kernel(h_ref, g_ref, wa_ref, wb_ref, cwa_ref, cwb_ref, cba_ref, cbb_ref,
                wd_ref, gf_ref, o_ref, xa_ref, xb_ref, ca_ref, cb_ref,
                *, tiles_per_seq, n_chunks, final_norm):
    tm = h_ref.shape[0]
    i = pl.program_id(0)

    @pl.when(i % tiles_per_seq == 0)
    def _():
        ca_ref[...] = jnp.zeros_like(ca_ref)
        cb_ref[...] = jnp.zeros_like(cb_ref)

    h = h_ref[...]
    hn = _rmsnorm(h, g_ref[...]).astype(BF16)

    def conv(x_ref, carry_ref, c, up, cw, cb):
        x_ref[0:SUBLANES, :] = carry_ref[c]
        x_ref[SUBLANES:SUBLANES + tm, :] = up
        carry_ref[c] = up[tm - SUBLANES:, :]
        x1 = x_ref[pl.ds(SUBLANES - 1, tm), :]
        x2 = x_ref[pl.ds(SUBLANES - 2, tm), :]
        return cb + cw[0:1, :] * x2 + cw[1:2, :] * x1 + cw[2:3, :] * up

    acc = h
    for c in range(n_chunks):
        ua = _dot(hn, wa_ref[c])
        ub = _dot(hn, wb_ref[c])
        va = conv(xa_ref, ca_ref, c, ua, cwa_ref[c], cba_ref[c])
        vb = conv(xb_ref, cb_ref, c, ub, cwb_ref[c], cbb_ref[c])
        act = (jax.nn.gelu(va) * vb).astype(BF16)
        acc = acc + _dot(act, wd_ref[c])
    if final_norm:
        acc = _rmsnorm(acc, gf_ref[...])
    o_ref[...] = acc


def _ffn_block(h, g, w_up, conv_w, conv_b, w_down, g_final, *, seq, tm, fc, final_norm):
    t, d = h.shape
    dff = w_down.shape[0]
    nc = dff // fc
    kw = conv_w.shape[0]
    wa = w_up[:, :dff].reshape(d, nc, fc).transpose(1, 0, 2).astype(BF16)
    wb = w_up[:, dff:].reshape(d, nc, fc).transpose(1, 0, 2).astype(BF16)
    cwa = conv_w[:, :dff].reshape(kw, nc, fc).transpose(1, 0, 2)
    cwb = conv_w[:, dff:].reshape(kw, nc, fc).transpose(1, 0, 2)
    cba = conv_b[:dff].reshape(nc, 1, fc)
    cbb = conv_b[dff:].reshape(nc, 1, fc)
    wd = w_down.reshape(nc, fc, d).astype(BF16)

    def full(a):
        nd = a.ndim
        return pl.BlockSpec(a.shape, lambda i: (0,) * nd)

    row = pl.BlockSpec((tm, d), lambda i: (i, 0))
    g2 = g.reshape(1, d)
    gf2 = g_final.reshape(1, d)
    kern = functools.partial(_ffn_kernel, tiles_per_seq=seq // tm, n_chunks=nc,
                             final_norm=final_norm)
    return pl.pallas_call(
        kern,
        grid=(t // tm,),
        in_specs=[row, full(g2), full(wa), full(wb), full(cwa), full(cwb), full(cba),
                  full(cbb), full(wd), full(gf2)],
        out_specs=row,
        out_shape=jax.ShapeDtypeStruct((t, d), F32),
        scratch_shapes=[pltpu.VMEM((SUBLANES + tm, fc), F32),
                        pltpu.VMEM((SUBLANES + tm, fc), F32),
                        pltpu.VMEM((nc, SUBLANES, fc), F32),
                        pltpu.VMEM((nc, SUBLANES, fc), F32)],
        compiler_params=pltpu.CompilerParams(dimension_semantics=("arbitrary",),
                                             vmem_limit_bytes=VMEM_LIMIT_BYTES),
        name="conv_ffn",
    )(h, g2, wa, wb, cwa, cwb, cba, cbb, wd, gf2)


def _rg_kernel(h_ref, g_ref, win_ref, cw_ref, cb_ref, wax_ref, ba_ref, bx_ref, lam_ref,
               wout_ref, o_ref, xr_ref, a_ref, b_ref, hc_ref, *, tiles_per_seq, heads):
    tm, d = h_ref.shape
    bw = d // heads
    kw = cw_ref.shape[0]
    i = pl.program_id(0)

    @pl.when(i % tiles_per_seq == 0)
    def _():
        xr_ref[0:SUBLANES, :] = jnp.zeros((SUBLANES, d), F32)
        hc_ref[...] = jnp.zeros_like(hc_ref)

    h = h_ref[...]
    hn = _rmsnorm(h, g_ref[...]).astype(BF16)
    xg = _dot(hn, win_ref[...])
    gate = xg[:, d:]
    xr_ref[SUBLANES:SUBLANES + tm, :] = xg[:, :d]
    xr = cb_ref[...]
    for k in range(kw):
        xr = xr + cw_ref[k:k + 1, :] * xr_ref[pl.ds(SUBLANES - (kw - 1) + k, tm), :]
    xr_ref[0:SUBLANES, :] = xr_ref[tm:tm + SUBLANES, :]

    neg_c_sp = -RG_C * jax.nn.softplus(-lam_ref[...])
    for hd in range(heads):
        sl = slice(hd * bw, (hd + 1) * bw)
        xh = xr[:, sl]
        ax = _dot(xh.astype(BF16), wax_ref[hd])
        r = jax.nn.sigmoid(ax[:, :bw] + ba_ref[:, sl])
        ig = jax.nn.sigmoid(ax[:, bw:] + bx_ref[:, sl])
        log_a = r * neg_c_sp[:, sl]
        a_ref[:, sl] = jnp.exp(log_a)
        th = jnp.tanh(log_a)
        b_ref[:, sl] = jnp.sqrt(-2.0 * th / (1.0 - th)) * (ig * xr[:, sl])

    rows = _row_ids((SUBLANES, d))

    def scan_body(r, carry):
        row = pl.multiple_of(r * SUBLANES, SUBLANES)
        a = a_ref[pl.ds(row, SUBLANES), :]
        b = b_ref[pl.ds(row, SUBLANES), :]
        for k in (1, 2, 4):
            keep = rows >= k
            sa = jnp.where(keep, pltpu.roll(a, k, 0), 1.0)
            sb = jnp.where(keep, pltpu.roll(b, k, 0), 0.0)
            b = b + a * sb
            a = a * sa
        hs = b + a * carry
        b_ref[pl.ds(row, SUBLANES), :] = hs
        return jnp.broadcast_to(hs[SUBLANES - 1:SUBLANES, :], (SUBLANES, d))

    hc_ref[...] = lax.fori_loop(0, tm // SUBLANES, scan_body, hc_ref[...])

    y = (b_ref[...] * jax.nn.gelu(gate)).astype(BF16)
    o_ref[...] = h + _dot(y, wout_ref[...])


def _rg_block(h, g, w_in, conv_w, conv_b, w_a, b_a, w_x, b_x, lam, w_out, *, seq, tm):
    t, d = h.shape
    heads = w_a.shape[0]
    wax = jnp.concatenate([w_a, w_x], axis=-1).astype(BF16)

    def full(a):
        nd = a.ndim
        return pl.BlockSpec(a.shape, lambda i: (0,) * nd)

    row = pl.BlockSpec((tm, d), lambda i: (i, 0))
    args = (g.reshape(1, d), w_in.astype(BF16), conv_w, conv_b.reshape(1, d), wax,
            b_a.reshape(1, d), b_x.reshape(1, d), lam.reshape(1, d), w_out.astype(BF16))
    kern = functools.partial(_rg_kernel, tiles_per_seq=seq // tm, heads=heads)
    return pl.pallas_call(
        kern,
        grid=(t // tm,),
        in_specs=[row] + [full(a) for a in args],
        out_specs=row,
        out_shape=jax.ShapeDtypeStruct((t, d), F32),
        scratch_shapes=[pltpu.VMEM((SUBLANES + tm, d), F32),
                        pltpu.VMEM((tm, d), F32),
                        pltpu.VMEM((tm, d), F32),
                        pltpu.VMEM((SUBLANES, d), F32)],
        compiler_params=pltpu.CompilerParams(dimension_semantics=("arbitrary",),
                                             vmem_limit_bytes=VMEM_LIMIT_BYTES),
        name="rglru_mixer",
    )(h, *args)


def _s5_kernel(h_ref, g_ref, win_ref, bblk_ref, pw_ref, cblk_ref, d_ref, wglu_ref,
               wout_ref, o_ref, u_ref, st_ref, y_ref, hc_ref, *, tiles_per_seq, lane_blocks):
    tm, d = h_ref.shape
    half = st_ref.shape[1] // 2
    i = pl.program_id(0)

    @pl.when(i % tiles_per_seq == 0)
    def _():
        hc_ref[...] = jnp.zeros_like(hc_ref)

    h = h_ref[...]
    hn = _rmsnorm(h, g_ref[...]).astype(BF16)
    u_ref[...] = _dot(hn, win_ref[...])

    n_lc = half // LANES
    for j in range(lane_blocks):
        uj = u_ref[:, j * LANES:(j + 1) * LANES].astype(BF16)
        st_ref[...] = _dot(uj, bblk_ref[j])

        def scan_body(r, carry, j=j):
            row = pl.multiple_of(r * SUBLANES, SUBLANES)
            new = []
            for lc in range(n_lc):
                lr = slice(lc * LANES, (lc + 1) * LANES)
                li = slice(half + lc * LANES, half + (lc + 1) * LANES)
                re = st_ref[pl.ds(row, SUBLANES), lr]
                im = st_ref[pl.ds(row, SUBLANES), li]
                for s in range(3):
                    k = 1 << s
                    pr = pw_ref[j, 2 * s, :, lr]
                    pi = pw_ref[j, 2 * s + 1, :, lr]
                    sre = pltpu.roll(re, k, 0)
                    sim = pltpu.roll(im, k, 0)
                    re, im = re + pr * sre - pi * sim, im + pr * sim + pi * sre
                qr = pw_ref[j, 6, :, lr]
                qi = pw_ref[j, 7, :, lr]
                cre, cim = carry[2 * lc], carry[2 * lc + 1]
                re, im = re + qr * cre - qi * cim, im + qr * cim + qi * cre
                st_ref[pl.ds(row, SUBLANES), lr] = re
                st_ref[pl.ds(row, SUBLANES), li] = im
                new.append(jnp.broadcast_to(re[SUBLANES - 1:, :], (SUBLANES, LANES)))
                new.append(jnp.broadcast_to(im[SUBLANES - 1:, :], (SUBLANES, LANES)))
            return tuple(new)

        init = tuple(hc_ref[j, :, q * LANES:(q + 1) * LANES] for q in range(2 * n_lc))
        fin = lax.fori_loop(0, tm // SUBLANES, scan_body, init)
        for q in range(2 * n_lc):
            hc_ref[j, :, q * LANES:(q + 1) * LANES] = fin[q]
        y_ref[:, j * LANES:(j + 1) * LANES] = _dot(st_ref[...].astype(BF16), cblk_ref[j])

    y = y_ref[...] + d_ref[...] * u_ref[...]
    gl = _dot(jax.nn.gelu(y).astype(BF16), wglu_ref[...])
    out = (gl[:, :d] * jax.nn.sigmoid(gl[:, d:])).astype(BF16)
    o_ref[...] = h + _dot(out, wout_ref[...])


def _cmul(x, y):
    return x[0] * y[0] - x[1] * y[1], x[0] * y[1] + x[1] * y[0]


def _s5_params(a_re, a_im, log_dt, b_re, b_im, c_re, c_im):
    g, p = a_re.shape
    gc = b_re.shape[-1]
    gpb = LANES // gc
    nb = g // gpb
    dt = jnp.exp(log_dt)[:, None]
    mag = jnp.exp(a_re * dt)
    abr = mag * jnp.cos(a_im * dt)
    abi = mag * jnp.sin(a_im * dt)
    ur, ui = abr - 1.0, abi
    den = a_re * a_re + a_im * a_im
    wr = (ur * a_re + ui * a_im) / den
    wi = (ui * a_re - ur * a_im) / den
    bbr = wr[..., None] * b_re - wi[..., None] * b_im
    bbi = wr[..., None] * b_im + wi[..., None] * b_re
    eye = jnp.eye(gpb, dtype=F32)
    bb = jnp.stack([bbr, bbi]).reshape(2, nb, gpb, p, gc)
    bblk = jnp.einsum('ab,zjapc->jaczbp', eye, bb).reshape(nb, gpb * gc, 2 * gpb * p)
    cc = jnp.stack([c_re, -c_im]).reshape(2, nb, gpb, gc, p)
    cblk = jnp.einsum('ab,zjacp->jzapbc', eye, cc).reshape(nb, 2 * gpb * p, gpb * gc)
    a1 = (abr.reshape(nb, 1, gpb * p), abi.reshape(nb, 1, gpb * p))
    pows = [a1]
    for _ in range(SUBLANES - 1):
        pows.append(_cmul(pows[-1], a1))
    rows = jnp.arange(SUBLANES)[None, :, None]
    planes = []
    for k in (1, 2, 4):
        for part in range(2):
            planes.append(jnp.where(rows >= k, pows[k - 1][part], 0.0))
    for part in range(2):
        planes.append(jnp.concatenate([pw[part] for pw in pows], axis=1))
    pw = jnp.stack(planes, axis=1)
    return bblk.astype(BF16), pw, cblk.astype(BF16)


def _s5_block(h, g, w_in, a_re, a_im, log_dt, b_re, b_im, c_re, c_im, dvec, w_glu, w_out,
              *, seq, tm):
    t, d = h.shape
    bblk, pw, cblk = _s5_params(a_re, a_im, log_dt, b_re, b_im, c_re, c_im)
    nb = bblk.shape[0]
    ns = bblk.shape[2]

    def full(a):
        nd = a.ndim
        return pl.BlockSpec(a.shape, lambda i: (0,) * nd)

    row = pl.BlockSpec((tm, d), lambda i: (i, 0))
    args = (g.reshape(1, d), w_in.astype(BF16), bblk, pw, cblk, dvec.reshape(1, d),
            w_glu.astype(BF16), w_out.astype(BF16))
    kern = functools.partial(_s5_kernel, tiles_per_seq=seq // tm, lane_blocks=nb)
    return pl.pallas_call(
        kern,
        grid=(t // tm,),
        in_specs=[row] + [full(a) for a in args],
        out_specs=row,
        out_shape=jax.ShapeDtypeStruct((t, d), F32),
        scratch_shapes=[pltpu.VMEM((tm, d), F32),
                        pltpu.VMEM((tm, ns), F32),
                        pltpu.VMEM((tm, d), F32),
                        pltpu.VMEM((nb, SUBLANES, ns), F32)],
        compiler_params=pltpu.CompilerParams(dimension_semantics=("arbitrary",),
                                             vmem_limit_bytes=VMEM_LIMIT_BYTES),
        name="s5_mixer",
    )(h, *args)


def kernel(x, norm_mix_g, norm_ffn_g, norm_final_g, rg_w_in, rg_conv_w, rg_conv_b, rg_w_a, rg_b_a, rg_w_x, rg_b_x, rg_lambda, rg_w_out, s5_w_in, s5_a_re, s5_a_im, s5_log_dt, s5_b_re, s5_b_im, s5_c_re, s5_c_im, s5_d, s5_w_glu, s5_w_out, ffn_w_up, ffn_conv_w, ffn_conv_b, ffn_w_down):
    bsz, seq, d = x.shape
    depth = norm_mix_g.shape[0]
    h = x.reshape(bsz * seq, d)
    for i in range(depth):
        j = i // 2
        if i % 2 == 0:
            h = _rg_block(h, norm_mix_g[i], rg_w_in[j], rg_conv_w[j], rg_conv_b[j], rg_w_a[j],
                          rg_b_a[j].reshape(-1), rg_w_x[j], rg_b_x[j].reshape(-1), rg_lambda[j],
                          rg_w_out[j], seq=seq, tm=256)
        else:
            h = _s5_block(h, norm_mix_g[i], s5_w_in[j], s5_a_re[j], s5_a_im[j], s5_log_dt[j],
                          s5_b_re[j], s5_b_im[j], s5_c_re[j], s5_c_im[j], s5_d[j], s5_w_glu[j],
                          s5_w_out[j], seq=seq, tm=256)
        h = _ffn_block(h, norm_ffn_g[i], ffn_w_up[i], ffn_conv_w[i], ffn_conv_b[i],
                       ffn_w_down[i], norm_final_g, seq=seq, tm=256, fc=512,
                       final_norm=(i == depth - 1))
    return h.reshape(bsz, seq, d)
```

```python
import functools

import jax
import jax.numpy as jnp
from jax import lax
from jax.experimental import pallas as pl
from jax.experimental.pallas import tpu as pltpu

NORM_EPS = 1e-6
RG_C = 8.0
SUBLANES = 8
LANES = 128
VMEM_LIMIT_BYTES = 56 * 1024 * 1024
TILE_ROWS = 256
FFN_CHUNK = 512

F32 = jnp.float32
BF16 = jnp.bfloat16


def _rmsnorm(x, g):
    var = jnp.mean(x * x, axis=-1, keepdims=True)
    return x * lax.rsqrt(var + NORM_EPS) * g


def _dot(a, b):
    return jnp.dot(a, b, preferred_element_type=F32)


def _is_row0(shape):
    return lax.broadcasted_iota(jnp.int32, shape, 0) == 0


def _prev_chunk_rows(cur, prev_tile):
    return jnp.where(_is_row0(cur.shape), pltpu.roll(prev_tile, 1, 0), pltpu.roll(cur, 1, 0))


def _bcast_last_row(x):
    return jnp.broadcast_to(x[SUBLANES - 1:SUBLANES, :], x.shape)


def _full_spec(a):
    nd = a.ndim
    return pl.BlockSpec(a.shape, lambda i: (0,) * nd)


def _to_tile_order(x2d, tm):
    t, d = x2d.shape
    return x2d.reshape(t // tm, SUBLANES, tm // SUBLANES, d).transpose(0, 2, 1, 3).reshape(t, d)


def _from_tile_order(x2d, tm):
    t, d = x2d.shape
    return x2d.reshape(t // tm, tm // SUBLANES, SUBLANES, d).transpose(0, 2, 1, 3).reshape(t, d)


def _ffn_kernel(h_ref, g_ref, wa_ref, wb_ref, cwa_ref, cwb_ref, cba_ref, cbb_ref,
                wd_ref, gf_ref, o_ref, xa_ref, xb_ref, ca_ref, cb_ref,
                *, tiles_per_seq, n_chunks, final_norm):
    tm = h_ref.shape[0]
    kw = cwa_ref.shape[1]
    halo = (kw - 1) * SUBLANES
    i = pl.program_id(0)

    @pl.when(i % tiles_per_seq == 0)
    def _():
        ca_ref[...] = jnp.zeros_like(ca_ref)
        cb_ref[...] = jnp.zeros_like(cb_ref)

    h = h_ref[...]
    hn = _rmsnorm(h, g_ref[...]).astype(BF16)

    def conv(x_ref, carry_ref, c, up, cw, cb):
        x_ref[c, halo:halo + tm, :] = up
        out = cb + cw[kw - 1:kw, :] * up
        for s in range(1, kw):
            cur = up[tm - s * SUBLANES:tm - (s - 1) * SUBLANES, :]
            x_ref[c, halo - s * SUBLANES:halo - (s - 1) * SUBLANES, :] = (
                _prev_chunk_rows(cur, carry_ref[c, s - 1]))
            carry_ref[c, s - 1] = cur
        for s in range(1, kw):
            out = out + cw[kw - 1 - s:kw - s, :] * x_ref[c, halo - s * SUBLANES:halo - s * SUBLANES + tm, :]
        return out

    acc = h
    for c in range(n_chunks):
        ua = _dot(hn, wa_ref[c])
        ub = _dot(hn, wb_ref[c])
        va = conv(xa_ref, ca_ref, c, ua, cwa_ref[c], cba_ref[c])
        vb = conv(xb_ref, cb_ref, c, ub, cwb_ref[c], cbb_ref[c])
        act = (jax.nn.gelu(va) * vb).astype(BF16)
        acc = acc + _dot(act, wd_ref[c])
    if final_norm:
        acc = _rmsnorm(acc, gf_ref[...])
    o_ref[...] = acc


def _ffn_block(h, g, w_up, conv_w, conv_b, w_down, g_final, *, seq, final_norm):
    t, d = h.shape
    tm, fc = TILE_ROWS, FFN_CHUNK
    dff = w_down.shape[0]
    nc = dff // fc
    kw = conv_w.shape[0]
    wa = w_up[:, :dff].reshape(d, nc, fc).transpose(1, 0, 2).astype(BF16)
    wb = w_up[:, dff:].reshape(d, nc, fc).transpose(1, 0, 2).astype(BF16)
    cwa = conv_w[:, :dff].reshape(kw, nc, fc).transpose(1, 0, 2)
    cwb = conv_w[:, dff:].reshape(kw, nc, fc).transpose(1, 0, 2)
    cba = conv_b[:dff].reshape(nc, 1, fc)
    cbb = conv_b[dff:].reshape(nc, 1, fc)
    wd = w_down.reshape(nc, fc, d).astype(BF16)

    row = pl.BlockSpec((tm, d), lambda i: (i, 0))
    args = (g.reshape(1, d), wa, wb, cwa, cwb, cba, cbb, wd, g_final.reshape(1, d))
    kern = functools.partial(_ffn_kernel, tiles_per_seq=seq // tm, n_chunks=nc,
                             final_norm=final_norm)
    halo = (kw - 1) * SUBLANES
    return pl.pallas_call(
        kern,
        grid=(t // tm,),
        in_specs=[row] + [_full_spec(a) for a in args],
        out_specs=row,
        out_shape=jax.ShapeDtypeStruct((t, d), F32),
        scratch_shapes=[pltpu.VMEM((nc, halo + tm, fc), F32),
                        pltpu.VMEM((nc, halo + tm, fc), F32),
                        pltpu.VMEM((nc, kw - 1, SUBLANES, fc), F32),
                        pltpu.VMEM((nc, kw - 1, SUBLANES, fc), F32)],
        compiler_params=pltpu.CompilerParams(dimension_semantics=("arbitrary",),
                                             vmem_limit_bytes=VMEM_LIMIT_BYTES),
        name="conv_ffn",
    )(h, *args)


def _rg_kernel(h_ref, g_ref, win_ref, cw_ref, cb_ref, wax_ref, ba_ref, bx_ref, lam_ref,
               wout_ref, o_ref, xr_ref, a_ref, b_ref, xc_ref, hc_ref, *, tiles_per_seq, heads):
    tm, d = h_ref.shape
    bw = d // heads
    kw = cw_ref.shape[0]
    halo = (kw - 1) * SUBLANES
    steps = tm // SUBLANES
    i = pl.program_id(0)

    @pl.when(i % tiles_per_seq == 0)
    def _():
        xc_ref[...] = jnp.zeros_like(xc_ref)
        hc_ref[...] = jnp.zeros_like(hc_ref)

    h = h_ref[...]
    hn = _rmsnorm(h, g_ref[...]).astype(BF16)
    xg = _dot(hn, win_ref[...])
    gate = xg[:, d:]
    xp = xg[:, :d]
    xr_ref[halo:halo + tm, :] = xp
    xr = cb_ref[...] + cw_ref[kw - 1:kw, :] * xp
    for s in range(1, kw):
        cur = xp[tm - s * SUBLANES:tm - (s - 1) * SUBLANES, :]
        xr_ref[halo - s * SUBLANES:halo - (s - 1) * SUBLANES, :] = _prev_chunk_rows(cur, xc_ref[s - 1])
        xc_ref[s - 1] = cur
    for s in range(1, kw):
        xr = xr + cw_ref[kw - 1 - s:kw - s, :] * xr_ref[halo - s * SUBLANES:halo - s * SUBLANES + tm, :]

    neg_c_sp = -RG_C * jax.nn.softplus(-lam_ref[...])
    for hd in range(heads):
        sl = slice(hd * bw, (hd + 1) * bw)
        xh = xr[:, sl]
        ax = _dot(xh.astype(BF16), wax_ref[hd])
        r = jax.nn.sigmoid(ax[:, :bw] + ba_ref[:, sl])
        ig = jax.nn.sigmoid(ax[:, bw:] + bx_ref[:, sl])
        log_a = r * neg_c_sp[:, sl]
        a_ref[:, sl] = jnp.exp(log_a)
        th = jnp.tanh(log_a)
        b_ref[:, sl] = jnp.sqrt(-2.0 * th / (1.0 - th)) * (ig * xh)

    def local_scan(tl, carry):
        hloc, p = carry
        row = pl.multiple_of(tl * SUBLANES, SUBLANES)
        a = a_ref[pl.ds(row, SUBLANES), :]
        hloc = a * hloc + b_ref[pl.ds(row, SUBLANES), :]
        p = a * p
        b_ref[pl.ds(row, SUBLANES), :] = hloc
        a_ref[pl.ds(row, SUBLANES), :] = p
        return hloc, p

    e, p = lax.fori_loop(0, steps, local_scan,
                         (jnp.zeros((SUBLANES, d), F32), jnp.ones((SUBLANES, d), F32)))
    rows = lax.broadcasted_iota(jnp.int32, (SUBLANES, d), 0)
    pe = p
    for k in (1, 2, 4):
        keep = rows >= k
        e = e + pe * jnp.where(keep, pltpu.roll(e, k, 0), 0.0)
        pe = pe * jnp.where(keep, pltpu.roll(pe, k, 0), 1.0)
    hin = hc_ref[...]
    e = e + pe * hin
    init = jnp.where(rows == 0, hin, pltpu.roll(e, 1, 0))
    hc_ref[...] = _bcast_last_row(e)

    hs = (b_ref[...].reshape(steps, SUBLANES, d)
          + a_ref[...].reshape(steps, SUBLANES, d) * init[None]).reshape(tm, d)
    y = (hs * jax.nn.gelu(gate)).astype(BF16)
    o_ref[...] = h + _dot(y, wout_ref[...])


def _rg_block(h, g, w_in, conv_w, conv_b, w_a, b_a, w_x, b_x, lam, w_out, *, seq):
    t, d = h.shape
    tm = TILE_ROWS
    heads = w_a.shape[0]
    kw = conv_w.shape[0]
    wax = jnp.concatenate([w_a, w_x], axis=-1).astype(BF16)
    row = pl.BlockSpec((tm, d), lambda i: (i, 0))
    args = (g.reshape(1, d), w_in.astype(BF16), conv_w, conv_b.reshape(1, d), wax,
            b_a.reshape(1, d), b_x.reshape(1, d), lam.reshape(1, d), w_out.astype(BF16))
    kern = functools.partial(_rg_kernel, tiles_per_seq=seq // tm, heads=heads)
    return pl.pallas_call(
        kern,
        grid=(t // tm,),
        in_specs=[row] + [_full_spec(a) for a in args],
        out_specs=row,
        out_shape=jax.ShapeDtypeStruct((t, d), F32),
        scratch_shapes=[pltpu.VMEM(((kw - 1) * SUBLANES + tm, d), F32),
                        pltpu.VMEM((tm, d), F32),
                        pltpu.VMEM((tm, d), F32),
                        pltpu.VMEM((kw - 1, SUBLANES, d), F32),
                        pltpu.VMEM((SUBLANES, d), F32)],
        compiler_params=pltpu.CompilerParams(dimension_semantics=("arbitrary",),
                                             vmem_limit_bytes=VMEM_LIMIT_BYTES),
        name="rglru_mixer",
    )(h, *args)


def _s5_kernel(h_ref, g_ref, win_ref, bblk_ref, tab_ref, pw_ref, cblk_ref, d_ref, wglu_ref,
               wout_ref, o_ref, u_ref, st_ref, y_ref, hc_ref, *, tiles_per_seq, lane_blocks):
    tm, d = h_ref.shape
    half = st_ref.shape[1] // 2
    n_lc = half // LANES
    steps = tm // SUBLANES
    i = pl.program_id(0)

    @pl.when(i % tiles_per_seq == 0)
    def _():
        hc_ref[...] = jnp.zeros_like(hc_ref)

    h = h_ref[...]
    hn = _rmsnorm(h, g_ref[...]).astype(BF16)
    u_ref[...] = _dot(hn, win_ref[...])
    row0 = _is_row0((SUBLANES, LANES))

    def lanes(lc):
        return (slice(lc * LANES, (lc + 1) * LANES),
                slice(half + lc * LANES, half + (lc + 1) * LANES))

    for j in range(lane_blocks):
        uj = u_ref[:, j * LANES:(j + 1) * LANES].astype(BF16)
        st_ref[...] = _dot(uj, bblk_ref[j])

        abar = []
        for lc in range(n_lc):
            lr, li = lanes(lc)
            abar.append((jnp.broadcast_to(tab_ref[j, 2 * lc, 0:1, :], (SUBLANES, LANES)),
                         jnp.broadcast_to(tab_ref[j, 2 * lc + 1, 0:1, :], (SUBLANES, LANES))))

        def local_scan(tl, carry, abar=abar):
            row = pl.multiple_of(tl * SUBLANES, SUBLANES)
            new = []
            for lc in range(n_lc):
                lr, li = lanes(lc)
                ar, ai = abar[lc]
                hr, hi = carry[2 * lc], carry[2 * lc + 1]
                nr = ar * hr - ai * hi + st_ref[pl.ds(row, SUBLANES), lr]
                ni = ar * hi + ai * hr + st_ref[pl.ds(row, SUBLANES), li]
                st_ref[pl.ds(row, SUBLANES), lr] = nr
                st_ref[pl.ds(row, SUBLANES), li] = ni
                new += [nr, ni]
            return tuple(new)

        zero = jnp.zeros((SUBLANES, LANES), F32)
        ends = lax.fori_loop(0, steps, local_scan, (zero,) * (2 * n_lc))

        inits = []
        for lc in range(n_lc):
            lr, li = lanes(lc)
            er, ei = ends[2 * lc], ends[2 * lc + 1]
            for s in range(3):
                k = 1 << s
                pr = pw_ref[j, 2 * s, :, lr]
                pi = pw_ref[j, 2 * s + 1, :, lr]
                sr = pltpu.roll(er, k, 0)
                si = pltpu.roll(ei, k, 0)
                er, ei = er + pr * sr - pi * si, ei + pr * si + pi * sr
            qr = pw_ref[j, 6, :, lr]
            qi = pw_ref[j, 7, :, lr]
            cr = hc_ref[j, :, lr]
            ci = hc_ref[j, :, li]
            er, ei = er + qr * cr - qi * ci, ei + qr * ci + qi * cr
            inits.append((jnp.where(row0, cr, pltpu.roll(er, 1, 0)),
                          jnp.where(row0, ci, pltpu.roll(ei, 1, 0))))
            hc_ref[j, :, lr] = _bcast_last_row(er)
            hc_ref[j, :, li] = _bcast_last_row(ei)

        def fixup(tl, c, inits=inits, j=j):
            row = pl.multiple_of(tl * SUBLANES, SUBLANES)
            for lc in range(n_lc):
                lr, li = lanes(lc)
                tr = tab_ref[j, 2 * lc, pl.ds(tl, SUBLANES, stride=0), :]
                ti = tab_ref[j, 2 * lc + 1, pl.ds(tl, SUBLANES, stride=0), :]
                zr, zi = inits[lc]
                st_ref[pl.ds(row, SUBLANES), lr] += tr * zr - ti * zi
                st_ref[pl.ds(row, SUBLANES), li] += tr * zi + ti * zr
            return c

        lax.fori_loop(0, steps, fixup, 0)
        y_ref[:, j * LANES:(j + 1) * LANES] = _dot(st_ref[...].astype(BF16), cblk_ref[j])

    y = y_ref[...] + d_ref[...] * u_ref[...]
    gl = _dot(jax.nn.gelu(y).astype(BF16), wglu_ref[...])
    out = (gl[:, :d] * jax.nn.sigmoid(gl[:, d:])).astype(BF16)
    o_ref[...] = h + _dot(out, wout_ref[...])


def _cmul(x, y):
    return x[0] * y[0] - x[1] * y[1], x[0] * y[1] + x[1] * y[0]


def _cpowers(a, n):
    pows = [a]
    while len(pows) < n:
        top = pows[-1]
        pows = pows + [_cmul(p, top) for p in pows]
    return pows[:n]


def _s5_params(a_re, a_im, log_dt, b_re, b_im, c_re, c_im, chunk):
    g, p = a_re.shape
    gc = b_re.shape[-1]
    gpb = LANES // gc
    nb = g // gpb
    dt = jnp.exp(log_dt)[:, None]
    mag = jnp.exp(a_re * dt)
    abr = mag * jnp.cos(a_im * dt)
    abi = mag * jnp.sin(a_im * dt)
    ur, ui = abr - 1.0, abi
    den = a_re * a_re + a_im * a_im
    wr = (ur * a_re + ui * a_im) / den
    wi = (ui * a_re - ur * a_im) / den
    bbr = wr[..., None] * b_re - wi[..., None] * b_im
    bbi = wr[..., None] * b_im + wi[..., None] * b_re
    eye = jnp.eye(gpb, dtype=F32)
    bb = jnp.stack([bbr, bbi]).reshape(2, nb, gpb, p, gc)
    bblk = jnp.einsum('ab,zjapc->jaczbp', eye, bb).reshape(nb, gpb * gc, 2 * gpb * p)
    cc = jnp.stack([c_re, -c_im]).reshape(2, nb, gpb, gc, p)
    cblk = jnp.einsum('ab,zjacp->jzapbc', eye, cc).reshape(nb, 2 * gpb * p, gpb * gc)
    a1 = (abr.reshape(nb, 1, gpb * p), abi.reshape(nb, 1, gpb * p))
    apow = _cpowers(a1, chunk)
    tab = jnp.stack([jnp.concatenate([pw[0] for pw in apow], axis=1),
                     jnp.concatenate([pw[1] for pw in apow], axis=1)], axis=1)
    tab = tab.reshape(nb, 2, chunk, gpb * p // LANES, LANES).transpose(0, 3, 1, 2, 4)
    tab = tab.reshape(nb, 2 * gpb * p // LANES, chunk, LANES)
    wpow = _cpowers(apow[-1], SUBLANES)
    rows = jnp.arange(SUBLANES)[None, :, None]
    planes = []
    for k in (1, 2, 4):
        for part in range(2):
            planes.append(jnp.where(rows >= k, wpow[k - 1][part], 0.0))
    for part in range(2):
        planes.append(jnp.concatenate([pw[part] for pw in wpow], axis=1))
    pw = jnp.stack(planes, axis=1)
    return bblk.astype(BF16), tab, pw, cblk.astype(BF16)


def _s5_block(h, g, w_in, a_re, a_im, log_dt, b_re, b_im, c_re, c_im, dvec, w_glu, w_out,
              *, seq):
    t, d = h.shape
    tm = TILE_ROWS
    bblk, tab, pw, cblk = _s5_params(a_re, a_im, log_dt, b_re, b_im, c_re, c_im, tm // SUBLANES)
    nb = bblk.shape[0]
    ns = bblk.shape[2]
    row = pl.BlockSpec((tm, d), lambda i: (i, 0))
    args = (g.reshape(1, d), w_in.astype(BF16), bblk, tab, pw, cblk, dvec.reshape(1, d),
            w_glu.astype(BF16), w_out.astype(BF16))
    kern = functools.partial(_s5_kernel, tiles_per_seq=seq // tm, lane_blocks=nb)
    return pl.pallas_call(
        kern,
        grid=(t // tm,),
        in_specs=[row] + [_full_spec(a) for a in args],
        out_specs=row,
        out_shape=jax.ShapeDtypeStruct((t, d), F32),
        scratch_shapes=[pltpu.VMEM((tm, d), F32),
                        pltpu.VMEM((tm, ns), F32),
                        pltpu.VMEM((tm, d), F32),
                        pltpu.VMEM((nb, SUBLANES, ns), F32)],
        compiler_params=pltpu.CompilerParams(dimension_semantics=("arbitrary",),
                                             vmem_limit_bytes=VMEM_LIMIT_BYTES),
        name="s5_mixer",
    )(h, *args)


def kernel(x, norm_mix_g, norm_ffn_g, norm_final_g, rg_w_in, rg_conv_w, rg_conv_b, rg_w_a, rg_b_a, rg_w_x, rg_b_x, rg_lambda, rg_w_out, s5_w_in, s5_a_re, s5_a_im, s5_log_dt, s5_b_re, s5_b_im, s5_c_re, s5_c_im, s5_d, s5_w_glu, s5_w_out, ffn_w_up, ffn_conv_w, ffn_conv_b, ffn_w_down):
    bsz, seq, d = x.shape
    depth = norm_mix_g.shape[0]
    assert seq % TILE_ROWS == 0
    h = _to_tile_order(x.reshape(bsz * seq, d), TILE_ROWS)
    for i in range(depth):
        j = i // 2
        if i % 2 == 0:
            h = _rg_block(h, norm_mix_g[i], rg_w_in[j], rg_conv_w[j], rg_conv_b[j], rg_w_a[j],
                          rg_b_a[j].reshape(-1), rg_w_x[j], rg_b_x[j].reshape(-1), rg_lambda[j],
                          rg_w_out[j], seq=seq)
        else:
            h = _s5_block(h, norm_mix_g[i], s5_w_in[j], s5_a_re[j], s5_a_im[j], s5_log_dt[j],
                          s5_b_re[j], s5_b_im[j], s5_c_re[j], s5_c_im[j], s5_d[j], s5_w_glu[j],
                          s5_w_out[j], seq=seq)
        h = _ffn_block(h, norm_ffn_g[i], ffn_w_up[i], ffn_conv_w[i], ffn_conv_b[i],
                       ffn_w_down[i], norm_final_g, seq=seq, final_norm=(i == depth - 1))
    return _from_tile_order(h, TILE_ROWS).reshape(bsz, seq, d)
```

```python
import functools

import jax
import jax.numpy as jnp
from jax import lax
from jax.experimental import pallas as pl
from jax.experimental.pallas import tpu as pltpu

NORM_EPS = 1e-6
RG_C = 8.0
SUBLANES = 8
LANES = 128
VMEM_LIMIT_BYTES = 56 * 1024 * 1024
TILE_ROWS = 512
FFN_CHUNK = 512

F32 = jnp.float32
BF16 = jnp.bfloat16


def _rmsnorm(x, g):
    var = jnp.mean(x * x, axis=-1, keepdims=True)
    return x * lax.rsqrt(var + NORM_EPS) * g


def _dot(a, b):
    return jnp.dot(a, b, preferred_element_type=F32)


def _is_row0(shape):
    return lax.broadcasted_iota(jnp.int32, shape, 0) == 0


def _prev_chunk_rows(cur, prev_tile):
    return jnp.where(_is_row0(cur.shape), pltpu.roll(prev_tile, 1, 0), pltpu.roll(cur, 1, 0))


def _bcast_last_row(x):
    return jnp.broadcast_to(x[SUBLANES - 1:SUBLANES, :], x.shape)


def _full_spec(a):
    nd = a.ndim
    return pl.BlockSpec(a.shape, lambda i: (0,) * nd, pipeline_mode=pl.Buffered(1))


def _to_tile_order(x2d, tm):
    t, d = x2d.shape
    return x2d.reshape(t // tm, SUBLANES, tm // SUBLANES, d).transpose(0, 2, 1, 3).reshape(t, d)


def _from_tile_order(x2d, tm):
    t, d = x2d.shape
    return x2d.reshape(t // tm, tm // SUBLANES, SUBLANES, d).transpose(0, 2, 1, 3).reshape(t, d)


def _ffn_kernel(h_ref, g_ref, wa_ref, wb_ref, cwa_ref, cwb_ref, cba_ref, cbb_ref,
                wd_ref, gf_ref, o_ref, xa_ref, xb_ref, ca_ref, cb_ref, act_ref,
                *, tiles_per_seq, n_chunks, final_norm):
    tm = h_ref.shape[0]
    kw = cwa_ref.shape[1]
    halo = (kw - 1) * SUBLANES
    i = pl.program_id(0)

    @pl.when(i % tiles_per_seq == 0)
    def _():
        ca_ref[...] = jnp.zeros_like(ca_ref)
        cb_ref[...] = jnp.zeros_like(cb_ref)

    h = h_ref[...]
    hn = _rmsnorm(h, g_ref[...]).astype(BF16)

    def conv(x_ref, carry_ref, c, up, cw, cb):
        x_ref[c, halo:halo + tm, :] = up
        out = cb + cw[kw - 1:kw, :] * up
        for s in range(1, kw):
            cur = up[tm - s * SUBLANES:tm - (s - 1) * SUBLANES, :]
            x_ref[c, halo - s * SUBLANES:halo - (s - 1) * SUBLANES, :] = (
                _prev_chunk_rows(cur, carry_ref[c, s - 1]))
            carry_ref[c, s - 1] = cur
        for s in range(1, kw):
            out = out + cw[kw - 1 - s:kw - s, :] * x_ref[c, halo - s * SUBLANES:halo - s * SUBLANES + tm, :]
        return out

    fc = wa_ref.shape[2]
    for c in range(n_chunks):
        ua = _dot(hn, wa_ref[c])
        ub = _dot(hn, wb_ref[c])
        va = conv(xa_ref, ca_ref, c, ua, cwa_ref[c], cba_ref[c])
        vb = conv(xb_ref, cb_ref, c, ub, cwb_ref[c], cbb_ref[c])
        act_ref[:, c * fc:(c + 1) * fc] = (jax.nn.gelu(va) * vb).astype(BF16)
    acc = h + _dot(act_ref[...], wd_ref[...])
    if final_norm:
        acc = _rmsnorm(acc, gf_ref[...])
    o_ref[...] = acc


def _ffn_block(h, g, w_up, conv_w, conv_b, w_down, g_final, *, seq, final_norm):
    t, d = h.shape
    tm, fc = TILE_ROWS, FFN_CHUNK
    dff = w_down.shape[0]
    nc = dff // fc
    kw = conv_w.shape[0]
    wa = w_up[:, :dff].reshape(d, nc, fc).transpose(1, 0, 2).astype(BF16)
    wb = w_up[:, dff:].reshape(d, nc, fc).transpose(1, 0, 2).astype(BF16)
    cwa = conv_w[:, :dff].reshape(kw, nc, fc).transpose(1, 0, 2)
    cwb = conv_w[:, dff:].reshape(kw, nc, fc).transpose(1, 0, 2)
    cba = conv_b[:dff].reshape(nc, 1, fc)
    cbb = conv_b[dff:].reshape(nc, 1, fc)
    wd = w_down.astype(BF16)

    row = pl.BlockSpec((tm, d), lambda i: (i, 0))
    args = (g.reshape(1, d), wa, wb, cwa, cwb, cba, cbb, wd, g_final.reshape(1, d))
    kern = functools.partial(_ffn_kernel, tiles_per_seq=seq // tm, n_chunks=nc,
                             final_norm=final_norm)
    halo = (kw - 1) * SUBLANES
    return pl.pallas_call(
        kern,
        grid=(t // tm,),
        in_specs=[row] + [_full_spec(a) for a in args],
        out_specs=row,
        out_shape=jax.ShapeDtypeStruct((t, d), F32),
        scratch_shapes=[pltpu.VMEM((nc, halo + tm, fc), F32),
                        pltpu.VMEM((nc, halo + tm, fc), F32),
                        pltpu.VMEM((nc, kw - 1, SUBLANES, fc), F32),
                        pltpu.VMEM((nc, kw - 1, SUBLANES, fc), F32),
                        pltpu.VMEM((tm, dff), BF16)],
        compiler_params=pltpu.CompilerParams(dimension_semantics=("arbitrary",),
                                             vmem_limit_bytes=VMEM_LIMIT_BYTES),
        name="conv_ffn",
    )(h, *args)


def _rg_kernel(h_ref, g_ref, win_ref, cw_ref, cb_ref, wax_ref, ba_ref, bx_ref, lam_ref,
               wout_ref, o_ref, xr_ref, a_ref, b_ref, xc_ref, hc_ref, y_ref, *, tiles_per_seq,
               heads):
    tm, d = h_ref.shape
    bw = d // heads
    kw = cw_ref.shape[0]
    halo = (kw - 1) * SUBLANES
    steps = tm // SUBLANES
    i = pl.program_id(0)

    @pl.when(i % tiles_per_seq == 0)
    def _():
        xc_ref[...] = jnp.zeros_like(xc_ref)
        hc_ref[...] = jnp.zeros_like(hc_ref)

    h = h_ref[...]
    hn = _rmsnorm(h, g_ref[...]).astype(BF16)
    xg = _dot(hn, win_ref[...])
    gate = xg[:, d:]
    xp = xg[:, :d]
    xr_ref[halo:halo + tm, :] = xp
    xr = cb_ref[...] + cw_ref[kw - 1:kw, :] * xp
    for s in range(1, kw):
        cur = xp[tm - s * SUBLANES:tm - (s - 1) * SUBLANES, :]
        xr_ref[halo - s * SUBLANES:halo - (s - 1) * SUBLANES, :] = _prev_chunk_rows(cur, xc_ref[s - 1])
        xc_ref[s - 1] = cur
    for s in range(1, kw):
        xr = xr + cw_ref[kw - 1 - s:kw - s, :] * xr_ref[halo - s * SUBLANES:halo - s * SUBLANES + tm, :]

    neg_c_sp = -RG_C * jax.nn.softplus(-lam_ref[...])
    rows = lax.broadcasted_iota(jnp.int32, (SUBLANES, bw), 0)
    for hd in range(heads):
        sl = slice(hd * bw, (hd + 1) * bw)
        xh = xr[:, sl]
        ax = _dot(xh.astype(BF16), wax_ref[hd])
        r = jax.nn.sigmoid(ax[:, :bw] + ba_ref[:, sl])
        ig = jax.nn.sigmoid(ax[:, bw:] + bx_ref[:, sl])
        log_a = r * neg_c_sp[:, sl]
        a_ref[:, sl] = jnp.exp(log_a)
        th = jnp.tanh(log_a)
        b_ref[:, sl] = jnp.sqrt(-2.0 * th / (1.0 - th)) * (ig * xh)

        e = b_ref[0:SUBLANES, sl]
        p = a_ref[0:SUBLANES, sl]
        for tl in range(1, steps):
            blk = slice(tl * SUBLANES, (tl + 1) * SUBLANES)
            a = a_ref[blk, sl]
            e = a * e + b_ref[blk, sl]
            p = a * p
            b_ref[blk, sl] = e
            a_ref[blk, sl] = p
        for k in (1, 2, 4):
            keep = rows >= k
            e = e + p * jnp.where(keep, pltpu.roll(e, k, 0), 0.0)
            p = p * jnp.where(keep, pltpu.roll(p, k, 0), 1.0)
        hin = hc_ref[:, sl]
        e = e + p * hin
        init = jnp.where(rows == 0, hin, pltpu.roll(e, 1, 0))
        hc_ref[:, sl] = _bcast_last_row(e)

        hs = (b_ref[:, sl].reshape(steps, SUBLANES, bw)
              + a_ref[:, sl].reshape(steps, SUBLANES, bw) * init[None]).reshape(tm, bw)
        y_ref[:, sl] = (hs * jax.nn.gelu(gate[:, sl])).astype(BF16)
    o_ref[...] = h + _dot(y_ref[...], wout_ref[...])


def _rg_block(h, g, w_in, conv_w, conv_b, w_a, b_a, w_x, b_x, lam, w_out, *, seq):
    t, d = h.shape
    tm = TILE_ROWS
    heads = w_a.shape[0]
    kw = conv_w.shape[0]
    wax = jnp.concatenate([w_a, w_x], axis=-1).astype(BF16)
    row = pl.BlockSpec((tm, d), lambda i: (i, 0))
    args = (g.reshape(1, d), w_in.astype(BF16), conv_w, conv_b.reshape(1, d), wax,
            b_a.reshape(1, d), b_x.reshape(1, d), lam.reshape(1, d), w_out.astype(BF16))
    kern = functools.partial(_rg_kernel, tiles_per_seq=seq // tm, heads=heads)
    return pl.pallas_call(
        kern,
        grid=(t // tm,),
        in_specs=[row] + [_full_spec(a) for a in args],
        out_specs=row,
        out_shape=jax.ShapeDtypeStruct((t, d), F32),
        scratch_shapes=[pltpu.VMEM(((kw - 1) * SUBLANES + tm, d), F32),
                        pltpu.VMEM((tm, d), F32),
                        pltpu.VMEM((tm, d), F32),
                        pltpu.VMEM((kw - 1, SUBLANES, d), F32),
                        pltpu.VMEM((SUBLANES, d), F32),
                        pltpu.VMEM((tm, d), BF16)],
        compiler_params=pltpu.CompilerParams(dimension_semantics=("arbitrary",),
                                             vmem_limit_bytes=VMEM_LIMIT_BYTES),
        name="rglru_mixer",
    )(h, *args)


def _s5_kernel(h_ref, g_ref, win_ref, bblk_ref, tab_ref, pw_ref, cblk_ref, d_ref, wglu_ref,
               wout_ref, o_ref, u_ref, st_ref, sb_ref, y_ref, hc_ref, *, tiles_per_seq,
               lane_blocks):
    tm, d = h_ref.shape
    half = st_ref.shape[2] // 2
    n_lc = half // LANES
    steps = tm // SUBLANES
    i = pl.program_id(0)

    @pl.when(i % tiles_per_seq == 0)
    def _():
        hc_ref[...] = jnp.zeros_like(hc_ref)

    h = h_ref[...]
    hn = _rmsnorm(h, g_ref[...]).astype(BF16)
    u_ref[...] = _dot(hn, win_ref[...])
    row0 = _is_row0((SUBLANES, LANES))

    def lanes(lc):
        return (slice(lc * LANES, (lc + 1) * LANES),
                slice(half + lc * LANES, half + (lc + 1) * LANES))

    def project_in(j):
        uj = u_ref[:, j * LANES:(j + 1) * LANES].astype(BF16)
        st_ref[j % 2] = _dot(uj, bblk_ref[j])

    project_in(0)
    for j in range(lane_blocks):
        if j + 1 < lane_blocks:
            project_in(j + 1)
        st = st_ref.at[j % 2]
        sb = sb_ref.at[j % 2]

        for lc in range(n_lc):
            lr, li = lanes(lc)
            ar = jnp.broadcast_to(tab_ref[j, 2 * lc, 0:1, :], (SUBLANES, LANES))
            ai = jnp.broadcast_to(tab_ref[j, 2 * lc + 1, 0:1, :], (SUBLANES, LANES))
            er = st[0:SUBLANES, lr]
            ei = st[0:SUBLANES, li]
            for tl in range(1, steps):
                rows = slice(tl * SUBLANES, (tl + 1) * SUBLANES)
                er, ei = (ar * er - ai * ei + st[rows, lr], ar * ei + ai * er + st[rows, li])
                st[rows, lr] = er
                st[rows, li] = ei

            for s in range(3):
                k = 1 << s
                pr = pw_ref[j, 2 * s, :, lr]
                pi = pw_ref[j, 2 * s + 1, :, lr]
                sr = pltpu.roll(er, k, 0)
                si = pltpu.roll(ei, k, 0)
                er, ei = er + pr * sr - pi * si, ei + pr * si + pi * sr
            qr = pw_ref[j, 6, :, lr]
            qi = pw_ref[j, 7, :, lr]
            cr = hc_ref[j, :, lr]
            ci = hc_ref[j, :, li]
            er, ei = er + qr * cr - qi * ci, ei + qr * ci + qi * cr
            zr = jnp.where(row0, cr, pltpu.roll(er, 1, 0))
            zi = jnp.where(row0, ci, pltpu.roll(ei, 1, 0))
            hc_ref[j, :, lr] = _bcast_last_row(er)
            hc_ref[j, :, li] = _bcast_last_row(ei)

            for t2 in range(0, steps, 2):
                outs_r, outs_i = [], []
                for tl in (t2, t2 + 1):
                    rows = slice(tl * SUBLANES, (tl + 1) * SUBLANES)
                    tr = tab_ref[j, 2 * lc, pl.ds(tl, SUBLANES, stride=0), :]
                    ti = tab_ref[j, 2 * lc + 1, pl.ds(tl, SUBLANES, stride=0), :]
                    outs_r.append(st[rows, lr] + (tr * zr - ti * zi))
                    outs_i.append(st[rows, li] + (tr * zi + ti * zr))
                rows2 = slice(t2 * SUBLANES, (t2 + 2) * SUBLANES)
                sb[rows2, lr] = jnp.concatenate(outs_r, axis=0).astype(BF16)
                sb[rows2, li] = jnp.concatenate(outs_i, axis=0).astype(BF16)

        y_ref[:, j * LANES:(j + 1) * LANES] = _dot(sb[...], cblk_ref[j])

    y = y_ref[...] + d_ref[...] * u_ref[...]
    gl = _dot(jax.nn.gelu(y).astype(BF16), wglu_ref[...])
    out = (gl[:, :d] * jax.nn.sigmoid(gl[:, d:])).astype(BF16)
    o_ref[...] = h + _dot(out, wout_ref[...])


def _cmul(x, y):
    return x[0] * y[0] - x[1] * y[1], x[0] * y[1] + x[1] * y[0]


def _cpowers(a, n):
    pows = [a]
    while len(pows) < n:
        top = pows[-1]
        pows = pows + [_cmul(p, top) for p in pows]
    return pows[:n]


def _s5_params(a_re, a_im, log_dt, b_re, b_im, c_re, c_im, chunk):
    g, p = a_re.shape
    gc = b_re.shape[-1]
    gpb = LANES // gc
    nb = g // gpb
    dt = jnp.exp(log_dt)[:, None]
    mag = jnp.exp(a_re * dt)
    abr = mag * jnp.cos(a_im * dt)
    abi = mag * jnp.sin(a_im * dt)
    ur, ui = abr - 1.0, abi
    den = a_re * a_re + a_im * a_im
    wr = (ur * a_re + ui * a_im) / den
    wi = (ui * a_re - ur * a_im) / den
    bbr = wr[..., None] * b_re - wi[..., None] * b_im
    bbi = wr[..., None] * b_im + wi[..., None] * b_re
    eye = jnp.eye(gpb, dtype=F32)
    bb = jnp.stack([bbr, bbi]).reshape(2, nb, gpb, p, gc)
    bblk = jnp.einsum('ab,zjapc->jaczbp', eye, bb).reshape(nb, gpb * gc, 2 * gpb * p)
    cc = jnp.stack([c_re, -c_im]).reshape(2, nb, gpb, gc, p)
    cblk = jnp.einsum('ab,zjacp->jzapbc', eye, cc).reshape(nb, 2 * gpb * p, gpb * gc)
    a1 = (abr.reshape(nb, 1, gpb * p), abi.reshape(nb, 1, gpb * p))
    apow = _cpowers(a1, chunk)
    tab = jnp.stack([jnp.concatenate([pw[0] for pw in apow], axis=1),
                     jnp.concatenate([pw[1] for pw in apow], axis=1)], axis=1)
    tab = tab.reshape(nb, 2, chunk, gpb * p // LANES, LANES).transpose(0, 3, 1, 2, 4)
    tab = tab.reshape(nb, 2 * gpb * p // LANES, chunk, LANES)
    wpow = _cpowers(apow[-1], SUBLANES)
    rows = jnp.arange(SUBLANES)[None, :, None]
    planes = []
    for k in (1, 2, 4):
        for part in range(2):
            planes.append(jnp.where(rows >= k, wpow[k - 1][part], 0.0))
    for part in range(2):
        planes.append(jnp.concatenate([pw[part] for pw in wpow], axis=1))
    pw = jnp.stack(planes, axis=1)
    return bblk.astype(BF16), tab, pw, cblk.astype(BF16)


def _s5_block(h, g, w_in, a_re, a_im, log_dt, b_re, b_im, c_re, c_im, dvec, w_glu, w_out,
              *, seq):
    t, d = h.shape
    tm = TILE_ROWS
    bblk, tab, pw, cblk = _s5_params(a_re, a_im, log_dt, b_re, b_im, c_re, c_im, tm // SUBLANES)
    nb = bblk.shape[0]
    ns = bblk.shape[2]
    row = pl.BlockSpec((tm, d), lambda i: (i, 0))
    args = (g.reshape(1, d), w_in.astype(BF16), bblk, tab, pw, cblk, dvec.reshape(1, d),
            w_glu.astype(BF16), w_out.astype(BF16))
    kern = functools.partial(_s5_kernel, tiles_per_seq=seq // tm, lane_blocks=nb)
    return pl.pallas_call(
        kern,
        grid=(t // tm,),
        in_specs=[row] + [_full_spec(a) for a in args],
        out_specs=row,
        out_shape=jax.ShapeDtypeStruct((t, d), F32),
        scratch_shapes=[pltpu.VMEM((tm, d), F32),
                        pltpu.VMEM((2, tm, ns), F32),
                        pltpu.VMEM((2, tm, ns), BF16),
                        pltpu.VMEM((tm, d), F32),
                        pltpu.VMEM((nb, SUBLANES, ns), F32)],
        compiler_params=pltpu.CompilerParams(dimension_semantics=("arbitrary",),
                                             vmem_limit_bytes=VMEM_LIMIT_BYTES),
        name="s5_mixer",
    )(h, *args)


def kernel(x, norm_mix_g, norm_ffn_g, norm_final_g, rg_w_in, rg_conv_w, rg_conv_b, rg_w_a, rg_b_a, rg_w_x, rg_b_x, rg_lambda, rg_w_out, s5_w_in, s5_a_re, s5_a_im, s5_log_dt, s5_b_re, s5_b_im, s5_c_re, s5_c_im, s5_d, s5_w_glu, s5_w_out, ffn_w_up, ffn_conv_w, ffn_conv_b, ffn_w_down):
    bsz, seq, d = x.shape
    depth = norm_mix_g.shape[0]
    assert seq % TILE_ROWS == 0
    h = _to_tile_order(x.reshape(bsz * seq, d), TILE_ROWS)
    for i in range(depth):
        j = i // 2
        if i % 2 == 0:
            h = _rg_block(h, norm_mix_g[i], rg_w_in[j], rg_conv_w[j], rg_conv_b[j], rg_w_a[j],
                          rg_b_a[j].reshape(-1), rg_w_x[j], rg_b_x[j].reshape(-1), rg_lambda[j],
                          rg_w_out[j], seq=seq)
        else:
            h = _s5_block(h, norm_mix_g[i], s5_w_in[j], s5_a_re[j], s5_a_im[j], s5_log_dt[j],
                          s5_b_re[j], s5_b_im[j], s5_c_re[j], s5_c_im[j], s5_d[j], s5_w_glu[j],
                          s5_w_out[j], seq=seq)
        h = _ffn_block(h, norm_ffn_g[i], ffn_w_up[i], ffn_conv_w[i], ffn_conv_b[i],
                       ffn_w_down[i], norm_final_g, seq=seq, final_norm=(i == depth - 1))
    return _from_tile_order(h, TILE_ROWS).reshape(bsz, seq, d)
```

```python
import functools

import jax
import jax.numpy as jnp
from jax import lax
from jax.experimental import pallas as pl
from jax.experimental.pallas import tpu as pltpu

NORM_EPS = 1e-6
RG_C = 8.0
SUBLANES = 8
LANES = 128
VMEM_LIMIT_BYTES = 56 * 1024 * 1024
TILE_ROWS = 512
FFN_CHUNK = 512

F32 = jnp.float32
BF16 = jnp.bfloat16


def _rmsnorm(x, g):
    var = jnp.mean(x * x, axis=-1, keepdims=True)
    return x * lax.rsqrt(var + NORM_EPS) * g


def _dot(a, b):
    return jnp.dot(a, b, preferred_element_type=F32)


def _is_row0(shape):
    return lax.broadcasted_iota(jnp.int32, shape, 0) == 0


def _prev_chunk_rows(cur, prev_tile):
    return jnp.where(_is_row0(cur.shape), pltpu.roll(prev_tile, 1, 0), pltpu.roll(cur, 1, 0))


def _bcast_last_row(x):
    return jnp.broadcast_to(x[SUBLANES - 1:SUBLANES, :], x.shape)


def _full_spec(a):
    nd = a.ndim
    return pl.BlockSpec(a.shape, lambda i: (0,) * nd, pipeline_mode=pl.Buffered(1))


def _to_tile_order(x2d, tm):
    t, d = x2d.shape
    return x2d.reshape(t // tm, SUBLANES, tm // SUBLANES, d).transpose(0, 2, 1, 3).reshape(t, d)


def _from_tile_order(x2d, tm):
    t, d = x2d.shape
    return x2d.reshape(t // tm, tm // SUBLANES, SUBLANES, d).transpose(0, 2, 1, 3).reshape(t, d)


def _ffn_kernel(h_ref, g_ref, wa_ref, wb_ref, cwa_ref, cwb_ref, cba_ref, cbb_ref,
                wd_ref, gf_ref, o_ref, xa_ref, xb_ref, ca_ref, cb_ref, act_ref,
                *, tiles_per_seq, n_chunks, final_norm):
    tm = h_ref.shape[0]
    kw = cwa_ref.shape[1]
    halo = (kw - 1) * SUBLANES
    i = pl.program_id(0)

    @pl.when(i % tiles_per_seq == 0)
    def _():
        ca_ref[...] = jnp.zeros_like(ca_ref)
        cb_ref[...] = jnp.zeros_like(cb_ref)

    h = h_ref[...]
    hn = _rmsnorm(h, g_ref[...]).astype(BF16)

    def conv(x_ref, carry_ref, c, up, cw, cb):
        x_ref[c, halo:halo + tm, :] = up
        out = cb + cw[kw - 1:kw, :] * up
        for s in range(1, kw):
            cur = up[tm - s * SUBLANES:tm - (s - 1) * SUBLANES, :]
            x_ref[c, halo - s * SUBLANES:halo - (s - 1) * SUBLANES, :] = (
                _prev_chunk_rows(cur, carry_ref[c, s - 1]))
            carry_ref[c, s - 1] = cur
        for s in range(1, kw):
            out = out + cw[kw - 1 - s:kw - s, :] * x_ref[c, halo - s * SUBLANES:halo - s * SUBLANES + tm, :]
        return out

    fc = wa_ref.shape[2]
    for c in range(n_chunks):
        ua = _dot(hn, wa_ref[c])
        ub = _dot(hn, wb_ref[c])
        va = conv(xa_ref, ca_ref, c, ua, cwa_ref[c], cba_ref[c])
        vb = conv(xb_ref, cb_ref, c, ub, cwb_ref[c], cbb_ref[c])
        act_ref[:, c * fc:(c + 1) * fc] = (jax.nn.gelu(va) * vb).astype(BF16)
    acc = h + _dot(act_ref[...], wd_ref[...])
    if final_norm:
        acc = _rmsnorm(acc, gf_ref[...])
    o_ref[...] = acc


def _ffn_block(h, g, w_up, conv_w, conv_b, w_down, g_final, *, seq, final_norm):
    t, d = h.shape
    tm, fc = TILE_ROWS, FFN_CHUNK
    dff = w_down.shape[0]
    nc = dff // fc
    kw = conv_w.shape[0]
    wa = w_up[:, :dff].reshape(d, nc, fc).transpose(1, 0, 2).astype(BF16)
    wb = w_up[:, dff:].reshape(d, nc, fc).transpose(1, 0, 2).astype(BF16)
    cwa = conv_w[:, :dff].reshape(kw, nc, fc).transpose(1, 0, 2)
    cwb = conv_w[:, dff:].reshape(kw, nc, fc).transpose(1, 0, 2)
    cba = conv_b[:dff].reshape(nc, 1, fc)
    cbb = conv_b[dff:].reshape(nc, 1, fc)
    wd = w_down.astype(BF16)

    row = pl.BlockSpec((tm, d), lambda i: (i, 0))
    args = (g.reshape(1, d), wa, wb, cwa, cwb, cba, cbb, wd, g_final.reshape(1, d))
    kern = functools.partial(_ffn_kernel, tiles_per_seq=seq // tm, n_chunks=nc,
                             final_norm=final_norm)
    halo = (kw - 1) * SUBLANES
    return pl.pallas_call(
        kern,
        grid=(t // tm,),
        in_specs=[row] + [_full_spec(a) for a in args],
        out_specs=row,
        out_shape=jax.ShapeDtypeStruct((t, d), F32),
        scratch_shapes=[pltpu.VMEM((nc, halo + tm, fc), F32),
                        pltpu.VMEM((nc, halo + tm, fc), F32),
                        pltpu.VMEM((nc, kw - 1, SUBLANES, fc), F32),
                        pltpu.VMEM((nc, kw - 1, SUBLANES, fc), F32),
                        pltpu.VMEM((tm, dff), BF16)],
        compiler_params=pltpu.CompilerParams(dimension_semantics=("arbitrary",),
                                             vmem_limit_bytes=VMEM_LIMIT_BYTES),
        name="conv_ffn",
    )(h, *args)


def _rg_kernel(h_ref, g_ref, win_ref, cw_ref, cb_ref, wax_ref, ba_ref, bx_ref, lam_ref,
               wout_ref, o_ref, xr_ref, a_ref, b_ref, xc_ref, hc_ref, y_ref, *, tiles_per_seq,
               heads):
    tm, d = h_ref.shape
    bw = d // heads
    kw = cw_ref.shape[0]
    halo = (kw - 1) * SUBLANES
    steps = tm // SUBLANES
    i = pl.program_id(0)

    @pl.when(i % tiles_per_seq == 0)
    def _():
        xc_ref[...] = jnp.zeros_like(xc_ref)
        hc_ref[...] = jnp.zeros_like(hc_ref)

    h = h_ref[...]
    hn = _rmsnorm(h, g_ref[...]).astype(BF16)
    xg = _dot(hn, win_ref[...])
    gate = xg[:, d:]
    xp = xg[:, :d]
    xr_ref[halo:halo + tm, :] = xp
    xr = cb_ref[...] + cw_ref[kw - 1:kw, :] * xp
    for s in range(1, kw):
        cur = xp[tm - s * SUBLANES:tm - (s - 1) * SUBLANES, :]
        xr_ref[halo - s * SUBLANES:halo - (s - 1) * SUBLANES, :] = _prev_chunk_rows(cur, xc_ref[s - 1])
        xc_ref[s - 1] = cur
    for s in range(1, kw):
        xr = xr + cw_ref[kw - 1 - s:kw - s, :] * xr_ref[halo - s * SUBLANES:halo - s * SUBLANES + tm, :]

    neg_c_sp = -RG_C * jax.nn.softplus(-lam_ref[...])
    rows = lax.broadcasted_iota(jnp.int32, (SUBLANES, bw), 0)
    for hd in range(heads):
        sl = slice(hd * bw, (hd + 1) * bw)
        xh = xr[:, sl]
        ax = _dot(xh.astype(BF16), wax_ref[hd])
        r = jax.nn.sigmoid(ax[:, :bw] + ba_ref[:, sl])
        ig = jax.nn.sigmoid(ax[:, bw:] + bx_ref[:, sl])
        log_a = r * neg_c_sp[:, sl]
        a_ref[:, sl] = jnp.exp(log_a)
        th = jnp.tanh(log_a)
        b_ref[:, sl] = jnp.sqrt(-2.0 * th / (1.0 - th)) * (ig * xh)

        e = b_ref[0:SUBLANES, sl]
        p = a_ref[0:SUBLANES, sl]
        for tl in range(1, steps):
            blk = slice(tl * SUBLANES, (tl + 1) * SUBLANES)
            a = a_ref[blk, sl]
            e = a * e + b_ref[blk, sl]
            p = a * p
            b_ref[blk, sl] = e
            a_ref[blk, sl] = p
        for k in (1, 2, 4):
            keep = rows >= k
            e = e + p * jnp.where(keep, pltpu.roll(e, k, 0), 0.0)
            p = p * jnp.where(keep, pltpu.roll(p, k, 0), 1.0)
        hin = hc_ref[:, sl]
        e = e + p * hin
        init = jnp.where(rows == 0, hin, pltpu.roll(e, 1, 0))
        hc_ref[:, sl] = _bcast_last_row(e)

        hs = (b_ref[:, sl].reshape(steps, SUBLANES, bw)
              + a_ref[:, sl].reshape(steps, SUBLANES, bw) * init[None]).reshape(tm, bw)
        y_ref[:, sl] = (hs * jax.nn.gelu(gate[:, sl])).astype(BF16)
    o_ref[...] = h + _dot(y_ref[...], wout_ref[...])


def _rg_block(h, g, w_in, conv_w, conv_b, w_a, b_a, w_x, b_x, lam, w_out, *, seq):
    t, d = h.shape
    tm = TILE_ROWS
    heads = w_a.shape[0]
    kw = conv_w.shape[0]
    wax = jnp.concatenate([w_a, w_x], axis=-1).astype(BF16)
    row = pl.BlockSpec((tm, d), lambda i: (i, 0))
    args = (g.reshape(1, d), w_in.astype(BF16), conv_w, conv_b.reshape(1, d), wax,
            b_a.reshape(1, d), b_x.reshape(1, d), lam.reshape(1, d), w_out.astype(BF16))
    kern = functools.partial(_rg_kernel, tiles_per_seq=seq // tm, heads=heads)
    return pl.pallas_call(
        kern,
        grid=(t // tm,),
        in_specs=[row] + [_full_spec(a) for a in args],
        out_specs=row,
        out_shape=jax.ShapeDtypeStruct((t, d), F32),
        scratch_shapes=[pltpu.VMEM(((kw - 1) * SUBLANES + tm, d), F32),
                        pltpu.VMEM((tm, d), F32),
                        pltpu.VMEM((tm, d), F32),
                        pltpu.VMEM((kw - 1, SUBLANES, d), F32),
                        pltpu.VMEM((SUBLANES, d), F32),
                        pltpu.VMEM((tm, d), BF16)],
        compiler_params=pltpu.CompilerParams(dimension_semantics=("arbitrary",),
                                             vmem_limit_bytes=VMEM_LIMIT_BYTES),
        name="rglru_mixer",
    )(h, *args)


def _s5_kernel(h_ref, g_ref, win_ref, bblk_ref, tab_ref, pw_ref, cblk_ref, d_ref, wglu_ref,
               wout_ref, o_ref, u_ref, st_ref, sb_ref, y_ref, hc_ref, *, tiles_per_seq,
               lane_blocks):
    tm, d = h_ref.shape
    half = st_ref.shape[2] // 2
    n_lc = half // LANES
    steps = tm // SUBLANES
    i = pl.program_id(0)

    @pl.when(i % tiles_per_seq == 0)
    def _():
        hc_ref[...] = jnp.zeros_like(hc_ref)

    h = h_ref[...]
    hn = _rmsnorm(h, g_ref[...]).astype(BF16)
    u_ref[...] = _dot(hn, win_ref[...])
    row0 = _is_row0((SUBLANES, LANES))
    ns = 2 * half

    def lanes(lc):
        return (slice(lc * LANES, (lc + 1) * LANES),
                slice(half + lc * LANES, half + (lc + 1) * LANES))

    def project_in(j):
        uj = u_ref[:, j * LANES:(j + 1) * LANES].astype(BF16)
        st_ref[j % 2] = _dot(uj, bblk_ref[j])

    def pair(blocks):
        return jnp.concatenate(blocks, axis=0).astype(BF16)

    project_in(0)
    for j in range(lane_blocks):
        if j + 1 < lane_blocks:
            project_in(j + 1)
        st = st_ref.at[j % 2]
        sb = sb_ref.at[j % 2]

        for lc in range(n_lc):
            lr, li = lanes(lc)
            ar = jnp.broadcast_to(tab_ref[j, 2 * lc, 0:1, :], (SUBLANES, LANES))
            ai = jnp.broadcast_to(tab_ref[j, 2 * lc + 1, 0:1, :], (SUBLANES, LANES))
            er = ei = None
            for t2 in range(0, steps, 2):
                blk_r, blk_i = [], []
                for tl in (t2, t2 + 1):
                    rows = slice(tl * SUBLANES, (tl + 1) * SUBLANES)
                    if tl == 0:
                        er, ei = st[rows, lr], st[rows, li]
                    else:
                        er, ei = (ar * er - ai * ei + st[rows, lr], ar * ei + ai * er + st[rows, li])
                    blk_r.append(er)
                    blk_i.append(ei)
                rows2 = slice(t2 * SUBLANES, (t2 + 2) * SUBLANES)
                sb[rows2, lr] = pair(blk_r)
                sb[rows2, li] = pair(blk_i)

            for s in range(3):
                k = 1 << s
                pr = pw_ref[j, 2 * s, :, lr]
                pi = pw_ref[j, 2 * s + 1, :, lr]
                sr = pltpu.roll(er, k, 0)
                si = pltpu.roll(ei, k, 0)
                er, ei = er + pr * sr - pi * si, ei + pr * si + pi * sr
            qr = pw_ref[j, 6, :, lr]
            qi = pw_ref[j, 7, :, lr]
            cr = hc_ref[j, :, lr]
            ci = hc_ref[j, :, li]
            er, ei = er + qr * cr - qi * ci, ei + qr * ci + qi * cr
            zr = jnp.where(row0, cr, pltpu.roll(er, 1, 0))
            zi = jnp.where(row0, ci, pltpu.roll(ei, 1, 0))
            hc_ref[j, :, lr] = _bcast_last_row(er)
            hc_ref[j, :, li] = _bcast_last_row(ei)

            zr2 = pair([zr, zr])
            zi2 = pair([zi, zi])
            for t2 in range(0, steps, 2):
                tr2 = pair([tab_ref[j, 2 * lc, pl.ds(tl, SUBLANES, stride=0), :] for tl in (t2, t2 + 1)])
                ti2 = pair([tab_ref[j, 2 * lc + 1, pl.ds(tl, SUBLANES, stride=0), :] for tl in (t2, t2 + 1)])
                rows2 = slice(t2 * SUBLANES, (t2 + 2) * SUBLANES)
                sb[rows2, ns + lc * LANES:ns + (lc + 1) * LANES] = tr2 * zr2 - ti2 * zi2
                sb[rows2, ns + half + lc * LANES:ns + half + (lc + 1) * LANES] = tr2 * zi2 + ti2 * zr2

        y_ref[:, j * LANES:(j + 1) * LANES] = _dot(sb[...], cblk_ref[j])

    y = y_ref[...] + d_ref[...] * u_ref[...]
    gl = _dot(jax.nn.gelu(y).astype(BF16), wglu_ref[...])
    out = (gl[:, :d] * jax.nn.sigmoid(gl[:, d:])).astype(BF16)
    o_ref[...] = h + _dot(out, wout_ref[...])


def _cmul(x, y):
    return x[0] * y[0] - x[1] * y[1], x[0] * y[1] + x[1] * y[0]


def _cpowers(a, n):
    pr, pi = a
    while pr.shape[1] < n:
        tr, ti = _cmul((pr, pi), (pr[:, -1:], pi[:, -1:]))
        pr = jnp.concatenate([pr, tr], axis=1)
        pi = jnp.concatenate([pi, ti], axis=1)
    return pr[:, :n], pi[:, :n]


def _s5_params(a_re, a_im, log_dt, b_re, b_im, c_re, c_im, chunk):
    g, p = a_re.shape
    gc = b_re.shape[-1]
    gpb = LANES // gc
    nb = g // gpb
    dt = jnp.exp(log_dt)[:, None]
    mag = jnp.exp(a_re * dt)
    abr = mag * jnp.cos(a_im * dt)
    abi = mag * jnp.sin(a_im * dt)
    ur, ui = abr - 1.0, abi
    den = a_re * a_re + a_im * a_im
    wr = (ur * a_re + ui * a_im) / den
    wi = (ui * a_re - ur * a_im) / den
    bbr = wr[..., None] * b_re - wi[..., None] * b_im
    bbi = wr[..., None] * b_im + wi[..., None] * b_re
    eye = jnp.eye(gpb, dtype=F32)
    bb = jnp.stack([bbr, bbi]).reshape(2, nb, gpb, p, gc)
    bblk = jnp.einsum('ab,zjapc->jaczbp', eye, bb).reshape(nb, gpb * gc, 2 * gpb * p)
    cc = jnp.stack([c_re, -c_im]).reshape(2, nb, gpb, gc, p)
    cblk = jnp.einsum('ab,zjacp->jzapbc', eye, cc).reshape(nb, 2 * gpb * p, gpb * gc)
    a1 = (abr.reshape(nb, 1, gpb * p), abi.reshape(nb, 1, gpb * p))
    apow = _cpowers(a1, chunk)
    tab = jnp.stack(apow, axis=1)
    tab = tab.reshape(nb, 2, chunk, gpb * p // LANES, LANES).transpose(0, 3, 1, 2, 4)
    tab = tab.reshape(nb, 2 * gpb * p // LANES, chunk, LANES)
    wpow = _cpowers((apow[0][:, -1:], apow[1][:, -1:]), SUBLANES)
    rows = jnp.arange(SUBLANES)[None, :, None]
    planes = []
    for k in (1, 2, 4):
        for part in range(2):
            planes.append(jnp.where(rows >= k, wpow[part][:, k - 1:k], 0.0))
    for part in range(2):
        planes.append(wpow[part])
    pw = jnp.stack(planes, axis=1)
    cblk = jnp.concatenate([cblk, cblk], axis=1)
    return bblk.astype(BF16), tab, pw, cblk.astype(BF16)


def _s5_block(h, g, w_in, a_re, a_im, log_dt, b_re, b_im, c_re, c_im, dvec, w_glu, w_out,
              *, seq):
    t, d = h.shape
    tm = TILE_ROWS
    bblk, tab, pw, cblk = _s5_params(a_re, a_im, log_dt, b_re, b_im, c_re, c_im, tm // SUBLANES)
    nb = bblk.shape[0]
    ns = bblk.shape[2]
    row = pl.BlockSpec((tm, d), lambda i: (i, 0))
    args = (g.reshape(1, d), w_in.astype(BF16), bblk, tab, pw, cblk, dvec.reshape(1, d),
            w_glu.astype(BF16), w_out.astype(BF16))
    kern = functools.partial(_s5_kernel, tiles_per_seq=seq // tm, lane_blocks=nb)
    return pl.pallas_call(
        kern,
        grid=(t // tm,),
        in_specs=[row] + [_full_spec(a) for a in args],
        out_specs=row,
        out_shape=jax.ShapeDtypeStruct((t, d), F32),
        scratch_shapes=[pltpu.VMEM((tm, d), F32),
                        pltpu.VMEM((2, tm, ns), F32),
                        pltpu.VMEM((2, tm, 2 * ns), BF16),
                        pltpu.VMEM((tm, d), F32),
                        pltpu.VMEM((nb, SUBLANES, ns), F32)],
        compiler_params=pltpu.CompilerParams(dimension_semantics=("arbitrary",),
                                             vmem_limit_bytes=VMEM_LIMIT_BYTES),
        name="s5_mixer",
    )(h, *args)


def kernel(x, norm_mix_g, norm_ffn_g, norm_final_g, rg_w_in, rg_conv_w, rg_conv_b, rg_w_a, rg_b_a, rg_w_x, rg_b_x, rg_lambda, rg_w_out, s5_w_in, s5_a_re, s5_a_im, s5_log_dt, s5_b_re, s5_b_im, s5_c_re, s5_c_im, s5_d, s5_w_glu, s5_w_out, ffn_w_up, ffn_conv_w, ffn_conv_b, ffn_w_down):
    bsz, seq, d = x.shape
    depth = norm_mix_g.shape[0]
    assert seq % TILE_ROWS == 0
    h = _to_tile_order(x.reshape(bsz * seq, d), TILE_ROWS)
    for i in range(depth):
        j = i // 2
        if i % 2 == 0:
            h = _rg_block(h, norm_mix_g[i], rg_w_in[j], rg_conv_w[j], rg_conv_b[j], rg_w_a[j],
                          rg_b_a[j].reshape(-1), rg_w_x[j], rg_b_x[j].reshape(-1), rg_lambda[j],
                          rg_w_out[j], seq=seq)
        else:
            h = _s5_block(h, norm_mix_g[i], s5_w_in[j], s5_a_re[j], s5_a_im[j], s5_log_dt[j],
                          s5_b_re[j], s5_b_im[j], s5_c_re[j], s5_c_im[j], s5_d[j], s5_w_glu[j],
                          s5_w_out[j], seq=seq)
        h = _ffn_block(h, norm_ffn_g[i], ffn_w_up[i], ffn_conv_w[i], ffn_conv_b[i],
                       ffn_w_down[i], norm_final_g, seq=seq, final_norm=(i == depth - 1))
    return _from_tile_order(h, TILE_ROWS).reshape(bsz, seq, d)
```

```python
import functools

import jax
import jax.numpy as jnp
from jax import lax
from jax.experimental import pallas as pl
from jax.experimental.pallas import tpu as pltpu

NORM_EPS = 1e-6
RG_C = 8.0
SUBLANES = 8
LANES = 128
VMEM_LIMIT_BYTES = 56 * 1024 * 1024
TILE_ROWS = 512
FFN_CHUNK = 512
ROW_PAD = LANES

F32 = jnp.float32
BF16 = jnp.bfloat16


def _rmsnorm(x, g):
    var = jnp.mean(x * x, axis=-1, keepdims=True)
    return x * lax.rsqrt(var + NORM_EPS) * g


def _dot(a, b):
    return jnp.dot(a, b, preferred_element_type=F32)


def _is_row0(shape):
    return lax.broadcasted_iota(jnp.int32, shape, 0) == 0


def _prev_chunk_rows(cur, prev_tile):
    return jnp.where(_is_row0(cur.shape), pltpu.roll(prev_tile, 1, 0), pltpu.roll(cur, 1, 0))


def _bcast_last_row(x):
    return jnp.broadcast_to(x[SUBLANES - 1:SUBLANES, :], x.shape)


def _full_spec(a):
    nd = a.ndim
    return pl.BlockSpec(a.shape, lambda i: (0,) * nd, pipeline_mode=pl.Buffered(1))


def _to_tile_order(x2d, tm):
    t, d = x2d.shape
    return x2d.reshape(t // tm, SUBLANES, tm // SUBLANES, d).transpose(0, 2, 1, 3).reshape(t, d)


def _from_tile_order(x2d, tm):
    t, d = x2d.shape
    return x2d.reshape(t // tm, tm // SUBLANES, SUBLANES, d).transpose(0, 2, 1, 3).reshape(t, d)


def _ffn_kernel(h_ref, g_ref, wa_ref, wb_ref, cwa_ref, cwb_ref, cba_ref, cbb_ref,
                wd_ref, gf_ref, o_ref, xa_ref, xb_ref, ca_ref, cb_ref, act_ref,
                *, tiles_per_seq, n_chunks, final_norm):
    tm = h_ref.shape[0]
    kw = cwa_ref.shape[1]
    halo = (kw - 1) * SUBLANES
    i = pl.program_id(0)

    @pl.when(i % tiles_per_seq == 0)
    def _():
        ca_ref[...] = jnp.zeros_like(ca_ref)
        cb_ref[...] = jnp.zeros_like(cb_ref)

    h = h_ref[...]
    hn = _rmsnorm(h, g_ref[...]).astype(BF16)

    def conv(x_ref, carry_ref, c, up, cw, cb):
        x_ref[c, halo:halo + tm, :] = up
        out = cb + cw[kw - 1:kw, :] * up
        for s in range(1, kw):
            cur = up[tm - s * SUBLANES:tm - (s - 1) * SUBLANES, :]
            x_ref[c, halo - s * SUBLANES:halo - (s - 1) * SUBLANES, :] = (
                _prev_chunk_rows(cur, carry_ref[c, s - 1]))
            carry_ref[c, s - 1] = cur
        for s in range(1, kw):
            out = out + cw[kw - 1 - s:kw - s, :] * x_ref[c, halo - s * SUBLANES:halo - s * SUBLANES + tm, :]
        return out

    fc = wa_ref.shape[2]
    for c in range(n_chunks):
        ua = _dot(hn, wa_ref[c])
        ub = _dot(hn, wb_ref[c])
        va = conv(xa_ref, ca_ref, c, ua, cwa_ref[c], cba_ref[c])
        vb = conv(xb_ref, cb_ref, c, ub, cwb_ref[c], cbb_ref[c])
        act_ref[:, c * fc:(c + 1) * fc] = (jax.nn.gelu(va) * vb).astype(BF16)
    acc = h + _dot(act_ref[...], wd_ref[...])
    if final_norm:
        acc = _rmsnorm(acc, gf_ref[...])
    o_ref[...] = acc


def _ffn_block(h, g, w_up, conv_w, conv_b, w_down, g_final, *, seq, final_norm):
    t, d = h.shape
    tm, fc = TILE_ROWS, FFN_CHUNK
    dff = w_down.shape[0]
    nc = dff // fc
    kw = conv_w.shape[0]
    wa = w_up[:, :dff].reshape(d, nc, fc).transpose(1, 0, 2).astype(BF16)
    wb = w_up[:, dff:].reshape(d, nc, fc).transpose(1, 0, 2).astype(BF16)
    cwa = conv_w[:, :dff].reshape(kw, nc, fc).transpose(1, 0, 2)
    cwb = conv_w[:, dff:].reshape(kw, nc, fc).transpose(1, 0, 2)
    cba = conv_b[:dff].reshape(nc, 1, fc)
    cbb = conv_b[dff:].reshape(nc, 1, fc)
    wd = w_down.astype(BF16)

    row = pl.BlockSpec((tm, d), lambda i: (i, 0))
    args = (g.reshape(1, d), wa, wb, cwa, cwb, cba, cbb, wd, g_final.reshape(1, d))
    kern = functools.partial(_ffn_kernel, tiles_per_seq=seq // tm, n_chunks=nc,
                             final_norm=final_norm)
    halo = (kw - 1) * SUBLANES
    return pl.pallas_call(
        kern,
        grid=(t // tm,),
        in_specs=[row] + [_full_spec(a) for a in args],
        out_specs=row,
        out_shape=jax.ShapeDtypeStruct((t, d), F32),
        scratch_shapes=[pltpu.VMEM((nc, halo + tm, fc), F32),
                        pltpu.VMEM((nc, halo + tm, fc), F32),
                        pltpu.VMEM((nc, kw - 1, SUBLANES, fc), F32),
                        pltpu.VMEM((nc, kw - 1, SUBLANES, fc), F32),
                        pltpu.VMEM((tm, dff), BF16)],
        compiler_params=pltpu.CompilerParams(dimension_semantics=("arbitrary",),
                                             vmem_limit_bytes=VMEM_LIMIT_BYTES),
        name="conv_ffn",
    )(h, *args)


def _rg_kernel(h_ref, g_ref, win_ref, cw_ref, cb_ref, wax_ref, ba_ref, bx_ref, lam_ref,
               wout_ref, o_ref, xr_ref, a_ref, b_ref, xc_ref, hc_ref, y_ref, *, tiles_per_seq,
               heads):
    tm, d = h_ref.shape
    bw = d // heads
    kw = cw_ref.shape[0]
    halo = (kw - 1) * SUBLANES
    steps = tm // SUBLANES
    i = pl.program_id(0)

    @pl.when(i % tiles_per_seq == 0)
    def _():
        xc_ref[...] = jnp.zeros_like(xc_ref)
        hc_ref[...] = jnp.zeros_like(hc_ref)

    h = h_ref[...]
    hn = _rmsnorm(h, g_ref[...]).astype(BF16)
    xg = _dot(hn, win_ref[...])
    gate = xg[:, d:]
    xp = xg[:, :d]
    xr_ref[halo:halo + tm, :] = xp
    xr = cb_ref[...] + cw_ref[kw - 1:kw, :] * xp
    for s in range(1, kw):
        cur = xp[tm - s * SUBLANES:tm - (s - 1) * SUBLANES, :]
        xr_ref[halo - s * SUBLANES:halo - (s - 1) * SUBLANES, :] = _prev_chunk_rows(cur, xc_ref[s - 1])
        xc_ref[s - 1] = cur
    for s in range(1, kw):
        xr = xr + cw_ref[kw - 1 - s:kw - s, :] * xr_ref[halo - s * SUBLANES:halo - s * SUBLANES + tm, :]

    neg_c_sp = -RG_C * jax.nn.softplus(-lam_ref[...])
    rows = lax.broadcasted_iota(jnp.int32, (SUBLANES, bw), 0)
    for hd in range(heads):
        sl = slice(hd * bw, (hd + 1) * bw)
        xh = xr[:, sl]
        ax = _dot(xh.astype(BF16), wax_ref[hd])
        r = jax.nn.sigmoid(ax[:, :bw] + ba_ref[:, sl])
        ig = jax.nn.sigmoid(ax[:, bw:] + bx_ref[:, sl])
        log_a = r * neg_c_sp[:, sl]
        a_ref[:, sl] = jnp.exp(log_a)
        th = jnp.tanh(log_a)
        b_ref[:, sl] = jnp.sqrt(-2.0 * th / (1.0 - th)) * (ig * xh)

        e = b_ref[0:SUBLANES, sl]
        p = a_ref[0:SUBLANES, sl]
        for tl in range(1, steps):
            blk = slice(tl * SUBLANES, (tl + 1) * SUBLANES)
            a = a_ref[blk, sl]
            e = a * e + b_ref[blk, sl]
            p = a * p
            b_ref[blk, sl] = e
            a_ref[blk, sl] = p
        for k in (1, 2, 4):
            keep = rows >= k
            e = e + p * jnp.where(keep, pltpu.roll(e, k, 0), 0.0)
            p = p * jnp.where(keep, pltpu.roll(p, k, 0), 1.0)
        hin = hc_ref[:, sl]
        e = e + p * hin
        init = jnp.where(rows == 0, hin, pltpu.roll(e, 1, 0))
        hc_ref[:, sl] = _bcast_last_row(e)

        hs = (b_ref[:, sl].reshape(steps, SUBLANES, bw)
              + a_ref[:, sl].reshape(steps, SUBLANES, bw) * init[None]).reshape(tm, bw)
        y_ref[:, sl] = (hs * jax.nn.gelu(gate[:, sl])).astype(BF16)
    o_ref[...] = h + _dot(y_ref[...], wout_ref[...])


def _rg_block(h, g, w_in, conv_w, conv_b, w_a, b_a, w_x, b_x, lam, w_out, *, seq):
    t, d = h.shape
    tm = TILE_ROWS
    heads = w_a.shape[0]
    kw = conv_w.shape[0]
    wax = jnp.concatenate([w_a, w_x], axis=-1).astype(BF16)
    row = pl.BlockSpec((tm, d), lambda i: (i, 0))
    args = (g.reshape(1, d), w_in.astype(BF16), conv_w, conv_b.reshape(1, d), wax,
            b_a.reshape(1, d), b_x.reshape(1, d), lam.reshape(1, d), w_out.astype(BF16))
    kern = functools.partial(_rg_kernel, tiles_per_seq=seq // tm, heads=heads)
    return pl.pallas_call(
        kern,
        grid=(t // tm,),
        in_specs=[row] + [_full_spec(a) for a in args],
        out_specs=row,
        out_shape=jax.ShapeDtypeStruct((t, d), F32),
        scratch_shapes=[pltpu.VMEM(((kw - 1) * SUBLANES + tm, d), F32),
                        pltpu.VMEM((tm, d + ROW_PAD), F32),
                        pltpu.VMEM((tm, d + ROW_PAD), F32),
                        pltpu.VMEM((kw - 1, SUBLANES, d), F32),
                        pltpu.VMEM((SUBLANES, d), F32),
                        pltpu.VMEM((tm, d), BF16)],
        compiler_params=pltpu.CompilerParams(dimension_semantics=("arbitrary",),
                                             vmem_limit_bytes=VMEM_LIMIT_BYTES),
        name="rglru_mixer",
    )(h, *args)


def _s5_kernel(h_ref, g_ref, win_ref, bblk_ref, tab_ref, pw_ref, cblk_ref, d_ref, wglu_ref,
               wout_ref, o_ref, u_ref, st_ref, sb_ref, y_ref, hc_ref, *, tiles_per_seq,
               lane_blocks):
    tm, d = h_ref.shape
    half = bblk_ref.shape[2] // 2
    n_lc = half // LANES
    steps = tm // SUBLANES
    i = pl.program_id(0)

    @pl.when(i % tiles_per_seq == 0)
    def _():
        hc_ref[...] = jnp.zeros_like(hc_ref)

    h = h_ref[...]
    hn = _rmsnorm(h, g_ref[...]).astype(BF16)
    u_ref[...] = _dot(hn, win_ref[...])
    row0 = _is_row0((SUBLANES, LANES))
    ns = 2 * half

    def lanes(lc):
        return (slice(lc * LANES, (lc + 1) * LANES),
                slice(half + lc * LANES, half + (lc + 1) * LANES))

    def project_in(j):
        uj = u_ref[:, j * LANES:(j + 1) * LANES].astype(BF16)
        st_ref[j % 2, :, 0:ns] = _dot(uj, bblk_ref[j])

    def pair(blocks):
        return jnp.concatenate(blocks, axis=0).astype(BF16)

    project_in(0)
    for j in range(lane_blocks):
        if j + 1 < lane_blocks:
            project_in(j + 1)
        st = st_ref.at[j % 2]
        sb = sb_ref.at[j % 2]

        for lc in range(n_lc):
            lr, li = lanes(lc)
            ar = jnp.broadcast_to(tab_ref[j, 2 * lc, 0:1, :], (SUBLANES, LANES))
            ai = jnp.broadcast_to(tab_ref[j, 2 * lc + 1, 0:1, :], (SUBLANES, LANES))
            er = ei = None
            for t2 in range(0, steps, 2):
                blk_r, blk_i = [], []
                for tl in (t2, t2 + 1):
                    rows = slice(tl * SUBLANES, (tl + 1) * SUBLANES)
                    if tl == 0:
                        er, ei = st[rows, lr], st[rows, li]
                    else:
                        er, ei = (ar * er - ai * ei + st[rows, lr], ar * ei + ai * er + st[rows, li])
                    blk_r.append(er)
                    blk_i.append(ei)
                rows2 = slice(t2 * SUBLANES, (t2 + 2) * SUBLANES)
                sb[rows2, lr] = pair(blk_r)
                sb[rows2, li] = pair(blk_i)

            for s in range(3):
                k = 1 << s
                pr = pw_ref[j, 2 * s, :, lr]
                pi = pw_ref[j, 2 * s + 1, :, lr]
                sr = pltpu.roll(er, k, 0)
                si = pltpu.roll(ei, k, 0)
                er, ei = er + pr * sr - pi * si, ei + pr * si + pi * sr
            qr = pw_ref[j, 6, :, lr]
            qi = pw_ref[j, 7, :, lr]
            cr = hc_ref[j, :, lr]
            ci = hc_ref[j, :, li]
            er, ei = er + qr * cr - qi * ci, ei + qr * ci + qi * cr
            zr = jnp.where(row0, cr, pltpu.roll(er, 1, 0))
            zi = jnp.where(row0, ci, pltpu.roll(ei, 1, 0))
            hc_ref[j, :, lr] = _bcast_last_row(er)
            hc_ref[j, :, li] = _bcast_last_row(ei)

            zr2 = pair([zr, zr])
            zi2 = pair([zi, zi])
            for t2 in range(0, steps, 2):
                tr2 = pair([tab_ref[j, 2 * lc, pl.ds(tl, SUBLANES, stride=0), :] for tl in (t2, t2 + 1)])
                ti2 = pair([tab_ref[j, 2 * lc + 1, pl.ds(tl, SUBLANES, stride=0), :] for tl in (t2, t2 + 1)])
                rows2 = slice(t2 * SUBLANES, (t2 + 2) * SUBLANES)
                sb[rows2, ns + lc * LANES:ns + (lc + 1) * LANES] = tr2 * zr2 - ti2 * zi2
                sb[rows2, ns + half + lc * LANES:ns + half + (lc + 1) * LANES] = tr2 * zi2 + ti2 * zr2

        y_ref[:, j * LANES:(j + 1) * LANES] = _dot(sb[:, 0:2 * ns], cblk_ref[j])

    y = y_ref[...] + d_ref[...] * u_ref[...]
    gl = _dot(jax.nn.gelu(y).astype(BF16), wglu_ref[...])
    out = (gl[:, :d] * jax.nn.sigmoid(gl[:, d:])).astype(BF16)
    o_ref[...] = h + _dot(out, wout_ref[...])


def _cmul(x, y):
    return x[0] * y[0] - x[1] * y[1], x[0] * y[1] + x[1] * y[0]


def _cpowers(a, n):
    pr, pi = a
    while pr.shape[1] < n:
        tr, ti = _cmul((pr, pi), (pr[:, -1:], pi[:, -1:]))
        pr = jnp.concatenate([pr, tr], axis=1)
        pi = jnp.concatenate([pi, ti], axis=1)
    return pr[:, :n], pi[:, :n]


def _s5_params(a_re, a_im, log_dt, b_re, b_im, c_re, c_im, chunk):
    g, p = a_re.shape
    gc = b_re.shape[-1]
    gpb = LANES // gc
    nb = g // gpb
    dt = jnp.exp(log_dt)[:, None]
    mag = jnp.exp(a_re * dt)
    abr = mag * jnp.cos(a_im * dt)
    abi = mag * jnp.sin(a_im * dt)
    ur, ui = abr - 1.0, abi
    den = a_re * a_re + a_im * a_im
    wr = (ur * a_re + ui * a_im) / den
    wi = (ui * a_re - ur * a_im) / den
    bbr = wr[..., None] * b_re - wi[..., None] * b_im
    bbi = wr[..., None] * b_im + wi[..., None] * b_re
    eye = jnp.eye(gpb, dtype=F32)
    bb = jnp.stack([bbr, bbi]).reshape(2, nb, gpb, p, gc)
    bblk = jnp.einsum('ab,zjapc->jaczbp', eye, bb).reshape(nb, gpb * gc, 2 * gpb * p)
    cc = jnp.stack([c_re, -c_im]).reshape(2, nb, gpb, gc, p)
    cblk = jnp.einsum('ab,zjacp->jzapbc', eye, cc).reshape(nb, 2 * gpb * p, gpb * gc)
    a1 = (abr.reshape(nb, 1, gpb * p), abi.reshape(nb, 1, gpb * p))
    apow = _cpowers(a1, chunk)
    tab = jnp.stack(apow, axis=1)
    tab = tab.reshape(nb, 2, chunk, gpb * p // LANES, LANES).transpose(0, 3, 1, 2, 4)
    tab = tab.reshape(nb, 2 * gpb * p // LANES, chunk, LANES)
    tab = jnp.pad(tab, ((0, 0), (0, 0), (0, SUBLANES), (0, 0)))
    wpow = _cpowers((apow[0][:, -1:], apow[1][:, -1:]), SUBLANES)
    rows = jnp.arange(SUBLANES)[None, :, None]
    planes = []
    for k in (1, 2, 4):
        for part in range(2):
            planes.append(jnp.where(rows >= k, wpow[part][:, k - 1:k], 0.0))
    for part in range(2):
        planes.append(wpow[part])
    pw = jnp.stack(planes, axis=1)
    cblk = jnp.concatenate([cblk, cblk], axis=1)
    return bblk.astype(BF16), tab, pw, cblk.astype(BF16)


def _s5_block(h, g, w_in, a_re, a_im, log_dt, b_re, b_im, c_re, c_im, dvec, w_glu, w_out,
              *, seq):
    t, d = h.shape
    tm = TILE_ROWS
    bblk, tab, pw, cblk = _s5_params(a_re, a_im, log_dt, b_re, b_im, c_re, c_im, tm // SUBLANES)
    nb = bblk.shape[0]
    ns = bblk.shape[2]
    row = pl.BlockSpec((tm, d), lambda i: (i, 0))
    args = (g.reshape(1, d), w_in.astype(BF16), bblk, tab, pw, cblk, dvec.reshape(1, d),
            w_glu.astype(BF16), w_out.astype(BF16))
    kern = functools.partial(_s5_kernel, tiles_per_seq=seq // tm, lane_blocks=nb)
    return pl.pallas_call(
        kern,
        grid=(t // tm,),
        in_specs=[row] + [_full_spec(a) for a in args],
        out_specs=row,
        out_shape=jax.ShapeDtypeStruct((t, d), F32),
        scratch_shapes=[pltpu.VMEM((tm, d), F32),
                        pltpu.VMEM((2, tm, ns + ROW_PAD), F32),
                        pltpu.VMEM((2, tm, 2 * ns + ROW_PAD), BF16),
                        pltpu.VMEM((tm, d), F32),
                        pltpu.VMEM((nb, SUBLANES, ns), F32)],
        compiler_params=pltpu.CompilerParams(dimension_semantics=("arbitrary",),
                                             vmem_limit_bytes=VMEM_LIMIT_BYTES),
        name="s5_mixer",
    )(h, *args)


def kernel(x, norm_mix_g, norm_ffn_g, norm_final_g, rg_w_in, rg_conv_w, rg_conv_b, rg_w_a, rg_b_a, rg_w_x, rg_b_x, rg_lambda, rg_w_out, s5_w_in, s5_a_re, s5_a_im, s5_log_dt, s5_b_re, s5_b_im, s5_c_re, s5_c_im, s5_d, s5_w_glu, s5_w_out, ffn_w_up, ffn_conv_w, ffn_conv_b, ffn_w_down):
    bsz, seq, d = x.shape
    depth = norm_mix_g.shape[0]
    assert seq % TILE_ROWS == 0
    h = _to_tile_order(x.reshape(bsz * seq, d), TILE_ROWS)
    for i in range(depth):
        j = i // 2
        if i % 2 == 0:
            h = _rg_block(h, norm_mix_g[i], rg_w_in[j], rg_conv_w[j], rg_conv_b[j], rg_w_a[j],
                          rg_b_a[j].reshape(-1), rg_w_x[j], rg_b_x[j].reshape(-1), rg_lambda[j],
                          rg_w_out[j], seq=seq)
        else:
            h = _s5_block(h, norm_mix_g[i], s5_w_in[j], s5_a_re[j], s5_a_im[j], s5_log_dt[j],
                          s5_b_re[j], s5_b_im[j], s5_c_re[j], s5_c_im[j], s5_d[j], s5_w_glu[j],
                          s5_w_out[j], seq=seq)
        h = _ffn_block(h, norm_ffn_g[i], ffn_w_up[i], ffn_conv_w[i], ffn_conv_b[i],
                       ffn_w_down[i], norm_final_g, seq=seq, final_norm=(i == depth - 1))
    return _from_tile_order(h, TILE_ROWS).reshape(bsz, seq, d)
```

```python
import functools

import jax
import jax.numpy as jnp
from jax import lax
from jax.experimental import pallas as pl
from jax.experimental.pallas import tpu as pltpu

NORM_EPS = 1e-6
RG_C = 8.0
SUBLANES = 8
LANES = 128
VMEM_LIMIT_BYTES = 56 * 1024 * 1024
TILE_ROWS = 512
FFN_CHUNK = 512
ROW_PAD = LANES

F32 = jnp.float32
BF16 = jnp.bfloat16


def _rmsnorm(x, g):
    var = jnp.mean(x * x, axis=-1, keepdims=True)
    return x * lax.rsqrt(var + NORM_EPS) * g


def _dot(a, b):
    return jnp.dot(a, b, preferred_element_type=F32)


def _is_row0(shape):
    return lax.broadcasted_iota(jnp.int32, shape, 0) == 0


def _prev_chunk_rows(cur, prev_tile):
    return jnp.where(_is_row0(cur.shape), pltpu.roll(prev_tile, 1, 0), pltpu.roll(cur, 1, 0))


def _bcast_last_row(x):
    return jnp.broadcast_to(x[SUBLANES - 1:SUBLANES, :], x.shape)


def _full_spec(a):
    nd = a.ndim
    return pl.BlockSpec(a.shape, lambda i: (0,) * nd, pipeline_mode=pl.Buffered(1))


def _to_tile_order(x2d, tm):
    t, d = x2d.shape
    return x2d.reshape(t // tm, SUBLANES, tm // SUBLANES, d).transpose(0, 2, 1, 3).reshape(t, d)


def _from_tile_order(x2d, tm):
    t, d = x2d.shape
    return x2d.reshape(t // tm, tm // SUBLANES, SUBLANES, d).transpose(0, 2, 1, 3).reshape(t, d)


def _ffn_kernel(h_ref, g_ref, wa_ref, wb_ref, cwa_ref, cwb_ref, cba_ref, cbb_ref,
                wd_ref, gf_ref, o_ref, xa_ref, xb_ref, ca_ref, cb_ref, act_ref,
                *, tiles_per_seq, n_chunks, final_norm):
    tm = h_ref.shape[0]
    kw = cwa_ref.shape[1]
    halo = (kw - 1) * SUBLANES
    i = pl.program_id(0)

    @pl.when(i % tiles_per_seq == 0)
    def _():
        ca_ref[...] = jnp.zeros_like(ca_ref)
        cb_ref[...] = jnp.zeros_like(cb_ref)

    h = h_ref[...]
    hn = _rmsnorm(h, g_ref[...]).astype(BF16)

    def conv(x_ref, carry_ref, c, up, cw, cb):
        x_ref[c, halo:halo + tm, :] = up
        out = cb + cw[kw - 1:kw, :] * up
        for s in range(1, kw):
            cur = up[tm - s * SUBLANES:tm - (s - 1) * SUBLANES, :]
            x_ref[c, halo - s * SUBLANES:halo - (s - 1) * SUBLANES, :] = (
                _prev_chunk_rows(cur, carry_ref[c, s - 1]))
            carry_ref[c, s - 1] = cur
        for s in range(1, kw):
            out = out + cw[kw - 1 - s:kw - s, :] * x_ref[c, halo - s * SUBLANES:halo - s * SUBLANES + tm, :]
        return out

    fc = wa_ref.shape[2]
    for c in range(n_chunks):
        ua = _dot(hn, wa_ref[c])
        ub = _dot(hn, wb_ref[c])
        va = conv(xa_ref, ca_ref, c, ua, cwa_ref[c], cba_ref[c])
        vb = conv(xb_ref, cb_ref, c, ub, cwb_ref[c], cbb_ref[c])
        act_ref[:, c * fc:(c + 1) * fc] = (jax.nn.gelu(va) * vb).astype(BF16)
    acc = h + _dot(act_ref[...], wd_ref[...])
    if final_norm:
        acc = _rmsnorm(acc, gf_ref[...])
    o_ref[...] = acc


def _ffn_block(h, g, w_up, conv_w, conv_b, w_down, g_final, *, seq, final_norm):
    t, d = h.shape
    tm, fc = TILE_ROWS, FFN_CHUNK
    dff = w_down.shape[0]
    nc = dff // fc
    kw = conv_w.shape[0]
    wa = w_up[:, :dff].reshape(d, nc, fc).transpose(1, 0, 2).astype(BF16)
    wb = w_up[:, dff:].reshape(d, nc, fc).transpose(1, 0, 2).astype(BF16)
    cwa = conv_w[:, :dff].reshape(kw, nc, fc).transpose(1, 0, 2)
    cwb = conv_w[:, dff:].reshape(kw, nc, fc).transpose(1, 0, 2)
    cba = conv_b[:dff].reshape(nc, 1, fc)
    cbb = conv_b[dff:].reshape(nc, 1, fc)
    wd = w_down.astype(BF16)

    row = pl.BlockSpec((tm, d), lambda i: (i, 0))
    args = (g.reshape(1, d), wa, wb, cwa, cwb, cba, cbb, wd, g_final.reshape(1, d))
    kern = functools.partial(_ffn_kernel, tiles_per_seq=seq // tm, n_chunks=nc,
                             final_norm=final_norm)
    halo = (kw - 1) * SUBLANES
    return pl.pallas_call(
        kern,
        grid=(t // tm,),
        in_specs=[row] + [_full_spec(a) for a in args],
        out_specs=row,
        out_shape=jax.ShapeDtypeStruct((t, d), F32),
        scratch_shapes=[pltpu.VMEM((nc, halo + tm, fc), F32),
                        pltpu.VMEM((nc, halo + tm, fc), F32),
                        pltpu.VMEM((nc, kw - 1, SUBLANES, fc), F32),
                        pltpu.VMEM((nc, kw - 1, SUBLANES, fc), F32),
                        pltpu.VMEM((tm, dff), BF16)],
        compiler_params=pltpu.CompilerParams(dimension_semantics=("arbitrary",),
                                             vmem_limit_bytes=VMEM_LIMIT_BYTES),
        name="conv_ffn",
    )(h, *args)


def _rg_kernel(h_ref, g_ref, win_ref, cw_ref, cb_ref, wax_ref, ba_ref, bx_ref, lam_ref,
               wout_ref, o_ref, xr_ref, a_ref, b_ref, xc_ref, hc_ref, y_ref, *, tiles_per_seq,
               heads):
    tm, d = h_ref.shape
    bw = d // heads
    kw = cw_ref.shape[0]
    halo = (kw - 1) * SUBLANES
    steps = tm // SUBLANES
    i = pl.program_id(0)

    @pl.when(i % tiles_per_seq == 0)
    def _():
        xc_ref[...] = jnp.zeros_like(xc_ref)
        hc_ref[...] = jnp.zeros_like(hc_ref)

    h = h_ref[...]
    hn = _rmsnorm(h, g_ref[...]).astype(BF16)
    xg = _dot(hn, win_ref[...])
    gate = xg[:, d:]
    xp = xg[:, :d]
    xr_ref[halo:halo + tm, :] = xp
    xr = cb_ref[...] + cw_ref[kw - 1:kw, :] * xp
    for s in range(1, kw):
        cur = xp[tm - s * SUBLANES:tm - (s - 1) * SUBLANES, :]
        xr_ref[halo - s * SUBLANES:halo - (s - 1) * SUBLANES, :] = _prev_chunk_rows(cur, xc_ref[s - 1])
        xc_ref[s - 1] = cur
    for s in range(1, kw):
        xr = xr + cw_ref[kw - 1 - s:kw - s, :] * xr_ref[halo - s * SUBLANES:halo - s * SUBLANES + tm, :]

    neg_c_sp = -RG_C * jax.nn.softplus(-lam_ref[...])
    rows = lax.broadcasted_iota(jnp.int32, (SUBLANES, bw), 0)
    for hd in range(heads):
        sl = slice(hd * bw, (hd + 1) * bw)
        xh = xr[:, sl]
        ax = _dot(xh.astype(BF16), wax_ref[hd])
        r = jax.nn.sigmoid(ax[:, :bw] + ba_ref[:, sl])
        ig = jax.nn.sigmoid(ax[:, bw:] + bx_ref[:, sl])
        log_a = r * neg_c_sp[:, sl]
        a_ref[:, sl] = jnp.exp(log_a)
        th = jnp.tanh(log_a)
        b_ref[:, sl] = jnp.sqrt(-2.0 * th / (1.0 - th)) * (ig * xh)

        e = b_ref[0:SUBLANES, sl]
        p = a_ref[0:SUBLANES, sl]
        for tl in range(1, steps):
            blk = slice(tl * SUBLANES, (tl + 1) * SUBLANES)
            a = a_ref[blk, sl]
            e = a * e + b_ref[blk, sl]
            p = a * p
            b_ref[blk, sl] = e
            a_ref[blk, sl] = p
        for k in (1, 2, 4):
            keep = rows >= k
            e = e + p * jnp.where(keep, pltpu.roll(e, k, 0), 0.0)
            p = p * jnp.where(keep, pltpu.roll(p, k, 0), 1.0)
        hin = hc_ref[:, sl]
        e = e + p * hin
        init = jnp.where(rows == 0, hin, pltpu.roll(e, 1, 0))
        hc_ref[:, sl] = _bcast_last_row(e)

        hs = (b_ref[:, sl].reshape(steps, SUBLANES, bw)
              + a_ref[:, sl].reshape(steps, SUBLANES, bw) * init[None]).reshape(tm, bw)
        y_ref[:, sl] = (hs * jax.nn.gelu(gate[:, sl])).astype(BF16)
    o_ref[...] = h + _dot(y_ref[...], wout_ref[...])


def _rg_block(h, g, w_in, conv_w, conv_b, w_a, b_a, w_x, b_x, lam, w_out, *, seq):
    t, d = h.shape
    tm = TILE_ROWS
    heads = w_a.shape[0]
    kw = conv_w.shape[0]
    wax = jnp.concatenate([w_a, w_x], axis=-1).astype(BF16)
    row = pl.BlockSpec((tm, d), lambda i: (i, 0))
    args = (g.reshape(1, d), w_in.astype(BF16), conv_w, conv_b.reshape(1, d), wax,
            b_a.reshape(1, d), b_x.reshape(1, d), lam.reshape(1, d), w_out.astype(BF16))
    kern = functools.partial(_rg_kernel, tiles_per_seq=seq // tm, heads=heads)
    return pl.pallas_call(
        kern,
        grid=(t // tm,),
        in_specs=[row] + [_full_spec(a) for a in args],
        out_specs=row,
        out_shape=jax.ShapeDtypeStruct((t, d), F32),
        scratch_shapes=[pltpu.VMEM(((kw - 1) * SUBLANES + tm, d), F32),
                        pltpu.VMEM((tm, d + ROW_PAD), F32),
                        pltpu.VMEM((tm, d + ROW_PAD), F32),
                        pltpu.VMEM((kw - 1, SUBLANES, d), F32),
                        pltpu.VMEM((SUBLANES, d), F32),
                        pltpu.VMEM((tm, d), BF16)],
        compiler_params=pltpu.CompilerParams(dimension_semantics=("arbitrary",),
                                             vmem_limit_bytes=VMEM_LIMIT_BYTES),
        name="rglru_mixer",
    )(h, *args)


def _s5_kernel(h_ref, g_ref, win_ref, bblk_ref, tab_ref, pw_ref, cblk_ref, d_ref, wglu_ref,
               wout_ref, o_ref, u_ref, st_ref, sb_ref, y_ref, hc_ref, *, tiles_per_seq,
               lane_blocks):
    tm, d = h_ref.shape
    half = bblk_ref.shape[2] // 2
    n_lc = half // LANES
    steps = tm // SUBLANES
    i = pl.program_id(0)

    @pl.when(i % tiles_per_seq == 0)
    def _():
        hc_ref[...] = jnp.zeros_like(hc_ref)

    h = h_ref[...]
    hn = _rmsnorm(h, g_ref[...]).astype(BF16)
    u_ref[...] = _dot(hn, win_ref[...])
    row0 = _is_row0((SUBLANES, LANES))
    ns = 2 * half

    def lanes(lc):
        return (slice(lc * LANES, (lc + 1) * LANES),
                slice(half + lc * LANES, half + (lc + 1) * LANES))

    def project_in(j):
        uj = u_ref[:, j * LANES:(j + 1) * LANES].astype(BF16)
        st_ref[j % 2, :, 0:ns] = _dot(uj, bblk_ref[j])

    def pair(blocks):
        return jnp.concatenate(blocks, axis=0).astype(BF16)

    project_in(0)
    for j in range(lane_blocks):
        if j + 1 < lane_blocks:
            project_in(j + 1)
        st = st_ref.at[j % 2]
        sb = sb_ref.at[(j // 2) % 2]
        off = (j % 2) * 2 * ns

        for lc in range(n_lc):
            lr, li = lanes(lc)
            ar = jnp.broadcast_to(tab_ref[j, 2 * lc, 0:1, :], (SUBLANES, LANES))
            ai = jnp.broadcast_to(tab_ref[j, 2 * lc + 1, 0:1, :], (SUBLANES, LANES))
            er = ei = None
            for t2 in range(0, steps, 2):
                blk_r, blk_i = [], []
                for tl in (t2, t2 + 1):
                    rows = slice(tl * SUBLANES, (tl + 1) * SUBLANES)
                    if tl == 0:
                        er, ei = st[rows, lr], st[rows, li]
                    else:
                        er, ei = (ar * er - ai * ei + st[rows, lr], ar * ei + ai * er + st[rows, li])
                    blk_r.append(er)
                    blk_i.append(ei)
                rows2 = slice(t2 * SUBLANES, (t2 + 2) * SUBLANES)
                sb[rows2, off + lc * LANES:off + (lc + 1) * LANES] = pair(blk_r)
                sb[rows2, off + half + lc * LANES:off + half + (lc + 1) * LANES] = pair(blk_i)

            for s in range(3):
                k = 1 << s
                pr = pw_ref[j, 2 * s, :, lr]
                pi = pw_ref[j, 2 * s + 1, :, lr]
                sr = pltpu.roll(er, k, 0)
                si = pltpu.roll(ei, k, 0)
                er, ei = er + pr * sr - pi * si, ei + pr * si + pi * sr
            qr = pw_ref[j, 6, :, lr]
            qi = pw_ref[j, 7, :, lr]
            cr = hc_ref[j, :, lr]
            ci = hc_ref[j, :, li]
            er, ei = er + qr * cr - qi * ci, ei + qr * ci + qi * cr
            zr = jnp.where(row0, cr, pltpu.roll(er, 1, 0))
            zi = jnp.where(row0, ci, pltpu.roll(ei, 1, 0))
            hc_ref[j, :, lr] = _bcast_last_row(er)
            hc_ref[j, :, li] = _bcast_last_row(ei)

            zr2 = pair([zr, zr])
            zi2 = pair([zi, zi])
            for t2 in range(0, steps, 2):
                tr2 = pair([tab_ref[j, 2 * lc, pl.ds(tl, SUBLANES, stride=0), :] for tl in (t2, t2 + 1)])
                ti2 = pair([tab_ref[j, 2 * lc + 1, pl.ds(tl, SUBLANES, stride=0), :] for tl in (t2, t2 + 1)])
                rows2 = slice(t2 * SUBLANES, (t2 + 2) * SUBLANES)
                fr = off + ns + lc * LANES
                fi = off + ns + half + lc * LANES
                sb[rows2, fr:fr + LANES] = tr2 * zr2 - ti2 * zi2
                sb[rows2, fi:fi + LANES] = tr2 * zi2 + ti2 * zr2

        if j % 2 == 1:
            jp = j // 2
            y_ref[:, jp * 2 * LANES:(jp + 1) * 2 * LANES] = _dot(sb[:, 0:4 * ns], cblk_ref[jp])

    y = y_ref[...] + d_ref[...] * u_ref[...]
    gl = _dot(jax.nn.gelu(y).astype(BF16), wglu_ref[...])
    out = (gl[:, :d] * jax.nn.sigmoid(gl[:, d:])).astype(BF16)
    o_ref[...] = h + _dot(out, wout_ref[...])


def _cmul(x, y):
    return x[0] * y[0] - x[1] * y[1], x[0] * y[1] + x[1] * y[0]


def _cpowers(a, n):
    pr, pi = a
    while pr.shape[1] < n:
        tr, ti = _cmul((pr, pi), (pr[:, -1:], pi[:, -1:]))
        pr = jnp.concatenate([pr, tr], axis=1)
        pi = jnp.concatenate([pi, ti], axis=1)
    return pr[:, :n], pi[:, :n]


def _s5_params(a_re, a_im, log_dt, b_re, b_im, c_re, c_im, chunk):
    g, p = a_re.shape
    gc = b_re.shape[-1]
    gpb = LANES // gc
    nb = g // gpb
    dt = jnp.exp(log_dt)[:, None]
    mag = jnp.exp(a_re * dt)
    abr = mag * jnp.cos(a_im * dt)
    abi = mag * jnp.sin(a_im * dt)
    ur, ui = abr - 1.0, abi
    den = a_re * a_re + a_im * a_im
    wr = (ur * a_re + ui * a_im) / den
    wi = (ui * a_re - ur * a_im) / den
    bbr = wr[..., None] * b_re - wi[..., None] * b_im
    bbi = wr[..., None] * b_im + wi[..., None] * b_re
    eye = jnp.eye(gpb, dtype=F32)
    bb = jnp.stack([bbr, bbi]).reshape(2, nb, gpb, p, gc)
    bblk = jnp.einsum('ab,zjapc->jaczbp', eye, bb).reshape(nb, gpb * gc, 2 * gpb * p)
    cc = jnp.stack([c_re, -c_im]).reshape(2, nb, gpb, gc, p)
    cblk = jnp.einsum('ab,zjacp->jzapbc', eye, cc).reshape(nb, 2 * gpb * p, gpb * gc)
    a1 = (abr.reshape(nb, 1, gpb * p), abi.reshape(nb, 1, gpb * p))
    apow = _cpowers(a1, chunk)
    tab = jnp.stack(apow, axis=1)
    tab = tab.reshape(nb, 2, chunk, gpb * p // LANES, LANES).transpose(0, 3, 1, 2, 4)
    tab = tab.reshape(nb, 2 * gpb * p // LANES, chunk, LANES)
    tab = jnp.pad(tab, ((0, 0), (0, 0), (0, SUBLANES), (0, 0)))
    wpow = _cpowers((apow[0][:, -1:], apow[1][:, -1:]), SUBLANES)
    rows = jnp.arange(SUBLANES)[None, :, None]
    planes = []
    for k in (1, 2, 4):
        for part in range(2):
            planes.append(jnp.where(rows >= k, wpow[part][:, k - 1:k], 0.0))
    for part in range(2):
        planes.append(wpow[part])
    pw = jnp.stack(planes, axis=1)
    cblk = jnp.concatenate([cblk, cblk], axis=1)
    zeros = jnp.zeros_like(cblk[0::2])
    cblk = jnp.concatenate([jnp.concatenate([cblk[0::2], zeros], axis=2),
                            jnp.concatenate([zeros, cblk[1::2]], axis=2)], axis=1)
    return bblk.astype(BF16), tab, pw, cblk.astype(BF16)


def _s5_block(h, g, w_in, a_re, a_im, log_dt, b_re, b_im, c_re, c_im, dvec, w_glu, w_out,
              *, seq):
    t, d = h.shape
    tm = TILE_ROWS
    bblk, tab, pw, cblk = _s5_params(a_re, a_im, log_dt, b_re, b_im, c_re, c_im, tm // SUBLANES)
    nb = bblk.shape[0]
    ns = bblk.shape[2]
    row = pl.BlockSpec((tm, d), lambda i: (i, 0))
    args = (g.reshape(1, d), w_in.astype(BF16), bblk, tab, pw, cblk, dvec.reshape(1, d),
            w_glu.astype(BF16), w_out.astype(BF16))
    kern = functools.partial(_s5_kernel, tiles_per_seq=seq // tm, lane_blocks=nb)
    return pl.pallas_call(
        kern,
        grid=(t // tm,),
        in_specs=[row] + [_full_spec(a) for a in args],
        out_specs=row,
        out_shape=jax.ShapeDtypeStruct((t, d), F32),
        scratch_shapes=[pltpu.VMEM((tm, d), F32),
                        pltpu.VMEM((2, tm, ns + ROW_PAD), F32),
                        pltpu.VMEM((2, tm, 4 * ns + ROW_PAD), BF16),
                        pltpu.VMEM((tm, d), F32),
                        pltpu.VMEM((nb, SUBLANES, ns), F32)],
        compiler_params=pltpu.CompilerParams(dimension_semantics=("arbitrary",),
                                             vmem_limit_bytes=VMEM_LIMIT_BYTES),
        name="s5_mixer",
    )(h, *args)


def kernel(x, norm_mix_g, norm_ffn_g, norm_final_g, rg_w_in, rg_conv_w, rg_conv_b, rg_w_a, rg_b_a, rg_w_x, rg_b_x, rg_lambda, rg_w_out, s5_w_in, s5_a_re, s5_a_im, s5_log_dt, s5_b_re, s5_b_im, s5_c_re, s5_c_im, s5_d, s5_w_glu, s5_w_out, ffn_w_up, ffn_conv_w, ffn_conv_b, ffn_w_down):
    bsz, seq, d = x.shape
    depth = norm_mix_g.shape[0]
    assert seq % TILE_ROWS == 0
    h = _to_tile_order(x.reshape(bsz * seq, d), TILE_ROWS)
    for i in range(depth):
        j = i // 2
        if i % 2 == 0:
            h = _rg_block(h, norm_mix_g[i], rg_w_in[j], rg_conv_w[j], rg_conv_b[j], rg_w_a[j],
                          rg_b_a[j].reshape(-1), rg_w_x[j], rg_b_x[j].reshape(-1), rg_lambda[j],
                          rg_w_out[j], seq=seq)
        else:
            h = _s5_block(h, norm_mix_g[i], s5_w_in[j], s5_a_re[j], s5_a_im[j], s5_log_dt[j],
                          s5_b_re[j], s5_b_im[j], s5_c_re[j], s5_c_im[j], s5_d[j], s5_w_glu[j],
                          s5_w_out[j], seq=seq)
        h = _ffn_block(h, norm_ffn_g[i], ffn_w_up[i], ffn_conv_w[i], ffn_conv_b[i],
                       ffn_w_down[i], norm_final_g, seq=seq, final_norm=(i == depth - 1))
    return _from_tile_order(h, TILE_ROWS).reshape(bsz, seq, d)
```

```python
import functools

import jax
import jax.numpy as jnp
from jax import lax
from jax.experimental import pallas as pl
from jax.experimental.pallas import tpu as pltpu

NORM_EPS = 1e-6
RG_C = 8.0
SUBLANES = 8
LANES = 128
VMEM_LIMIT_BYTES = 56 * 1024 * 1024
TILE_ROWS = 512
FFN_CHUNK = 512
ROW_PAD = LANES

F32 = jnp.float32
BF16 = jnp.bfloat16


def _rmsnorm(x, g):
    var = jnp.mean(x * x, axis=-1, keepdims=True)
    return x * lax.rsqrt(var + NORM_EPS) * g


def _sigmoid(x):
    return 0.5 * jnp.tanh(0.5 * x) + 0.5


def _dot(a, b):
    return jnp.dot(a, b, preferred_element_type=F32)


def _is_row0(shape):
    return lax.broadcasted_iota(jnp.int32, shape, 0) == 0


def _prev_chunk_rows(cur, prev_tile):
    return jnp.where(_is_row0(cur.shape), pltpu.roll(prev_tile, 1, 0), pltpu.roll(cur, 1, 0))


def _bcast_last_row(x):
    return jnp.broadcast_to(x[SUBLANES - 1:SUBLANES, :], x.shape)


def _full_spec(a):
    nd = a.ndim
    return pl.BlockSpec(a.shape, lambda i: (0,) * nd, pipeline_mode=pl.Buffered(1))


def _to_tile_order(x2d, tm):
    t, d = x2d.shape
    return x2d.reshape(t // tm, SUBLANES, tm // SUBLANES, d).transpose(0, 2, 1, 3).reshape(t, d)


def _from_tile_order(x2d, tm):
    t, d = x2d.shape
    return x2d.reshape(t // tm, tm // SUBLANES, SUBLANES, d).transpose(0, 2, 1, 3).reshape(t, d)


def _ffn_kernel(h_ref, g_ref, wa_ref, wb_ref, cwa_ref, cwb_ref, cba_ref, cbb_ref,
                wd_ref, gf_ref, o_ref, xa_ref, xb_ref, ca_ref, cb_ref, act_ref,
                *, tiles_per_seq, n_chunks, final_norm):
    tm = h_ref.shape[0]
    kw = cwa_ref.shape[1]
    halo = (kw - 1) * SUBLANES
    i = pl.program_id(0)

    @pl.when(i % tiles_per_seq == 0)
    def _():
        ca_ref[...] = jnp.zeros_like(ca_ref)
        cb_ref[...] = jnp.zeros_like(cb_ref)

    h = h_ref[...]
    hn = _rmsnorm(h, g_ref[...]).astype(BF16)

    def conv(x_ref, carry_ref, c, up, cw, cb):
        x_ref[c, halo:halo + tm, :] = up
        out = cb + cw[kw - 1:kw, :] * up
        for s in range(1, kw):
            cur = up[tm - s * SUBLANES:tm - (s - 1) * SUBLANES, :]
            x_ref[c, halo - s * SUBLANES:halo - (s - 1) * SUBLANES, :] = (
                _prev_chunk_rows(cur, carry_ref[c, s - 1]))
            carry_ref[c, s - 1] = cur
        for s in range(1, kw):
            out = out + cw[kw - 1 - s:kw - s, :] * x_ref[c, halo - s * SUBLANES:halo - s * SUBLANES + tm, :]
        return out

    fc = wa_ref.shape[2]
    for c in range(n_chunks):
        ua = _dot(hn, wa_ref[c])
        ub = _dot(hn, wb_ref[c])
        va = conv(xa_ref, ca_ref, c, ua, cwa_ref[c], cba_ref[c])
        vb = conv(xb_ref, cb_ref, c, ub, cwb_ref[c], cbb_ref[c])
        act_ref[:, c * fc:(c + 1) * fc] = (jax.nn.gelu(va) * vb).astype(BF16)
    acc = h + _dot(act_ref[...], wd_ref[...])
    if final_norm:
        acc = _rmsnorm(acc, gf_ref[...])
    o_ref[...] = acc


def _ffn_block(h, g, w_up, conv_w, conv_b, w_down, g_final, *, seq, final_norm):
    t, d = h.shape
    tm, fc = TILE_ROWS, FFN_CHUNK
    dff = w_down.shape[0]
    nc = dff // fc
    kw = conv_w.shape[0]
    wa = w_up[:, :dff].reshape(d, nc, fc).transpose(1, 0, 2).astype(BF16)
    wb = w_up[:, dff:].reshape(d, nc, fc).transpose(1, 0, 2).astype(BF16)
    cwa = conv_w[:, :dff].reshape(kw, nc, fc).transpose(1, 0, 2)
    cwb = conv_w[:, dff:].reshape(kw, nc, fc).transpose(1, 0, 2)
    cba = conv_b[:dff].reshape(nc, 1, fc)
    cbb = conv_b[dff:].reshape(nc, 1, fc)
    wd = w_down.astype(BF16)

    row = pl.BlockSpec((tm, d), lambda i: (i, 0))
    args = (g.reshape(1, d), wa, wb, cwa, cwb, cba, cbb, wd, g_final.reshape(1, d))
    kern = functools.partial(_ffn_kernel, tiles_per_seq=seq // tm, n_chunks=nc,
                             final_norm=final_norm)
    halo = (kw - 1) * SUBLANES
    return pl.pallas_call(
        kern,
        grid=(t // tm,),
        in_specs=[row] + [_full_spec(a) for a in args],
        out_specs=row,
        out_shape=jax.ShapeDtypeStruct((t, d), F32),
        scratch_shapes=[pltpu.VMEM((nc, halo + tm, fc), F32),
                        pltpu.VMEM((nc, halo + tm, fc), F32),
                        pltpu.VMEM((nc, kw - 1, SUBLANES, fc), F32),
                        pltpu.VMEM((nc, kw - 1, SUBLANES, fc), F32),
                        pltpu.VMEM((tm, dff), BF16)],
        compiler_params=pltpu.CompilerParams(dimension_semantics=("arbitrary",),
                                             vmem_limit_bytes=VMEM_LIMIT_BYTES),
        name="conv_ffn",
    )(h, *args)


def _rg_kernel(h_ref, g_ref, win_ref, cw_ref, cb_ref, wax_ref, ba_ref, bx_ref, lam_ref,
               wout_ref, o_ref, xr_ref, a_ref, b_ref, xc_ref, hc_ref, y_ref, *, tiles_per_seq,
               heads):
    tm, d = h_ref.shape
    bw = d // heads
    kw = cw_ref.shape[0]
    halo = (kw - 1) * SUBLANES
    steps = tm // SUBLANES
    i = pl.program_id(0)

    @pl.when(i % tiles_per_seq == 0)
    def _():
        xc_ref[...] = jnp.zeros_like(xc_ref)
        hc_ref[...] = jnp.zeros_like(hc_ref)

    h = h_ref[...]
    hn = _rmsnorm(h, g_ref[...]).astype(BF16)
    xg = _dot(hn, win_ref[...])
    gate = xg[:, d:]
    xp = xg[:, :d]
    xr_ref[halo:halo + tm, :] = xp
    xr = cb_ref[...] + cw_ref[kw - 1:kw, :] * xp
    for s in range(1, kw):
        cur = xp[tm - s * SUBLANES:tm - (s - 1) * SUBLANES, :]
        xr_ref[halo - s * SUBLANES:halo - (s - 1) * SUBLANES, :] = _prev_chunk_rows(cur, xc_ref[s - 1])
        xc_ref[s - 1] = cur
    for s in range(1, kw):
        xr = xr + cw_ref[kw - 1 - s:kw - s, :] * xr_ref[halo - s * SUBLANES:halo - s * SUBLANES + tm, :]

    neg_c_sp = -RG_C * jax.nn.softplus(-lam_ref[...])
    rows = lax.broadcasted_iota(jnp.int32, (SUBLANES, bw), 0)
    for hd in range(heads):
        sl = slice(hd * bw, (hd + 1) * bw)
        xh = xr[:, sl]
        ax = _dot(xh.astype(BF16), wax_ref[hd])
        r = _sigmoid(ax[:, :bw] + ba_ref[:, sl])
        ig = _sigmoid(ax[:, bw:] + bx_ref[:, sl])
        log_a = r * neg_c_sp[:, sl]
        a_ref[:, sl] = jnp.exp(log_a)
        th = jnp.tanh(log_a)
        b_ref[:, sl] = jnp.sqrt(-2.0 * th / (1.0 - th)) * (ig * xh)

        e = b_ref[0:SUBLANES, sl]
        p = a_ref[0:SUBLANES, sl]
        for tl in range(1, steps):
            blk = slice(tl * SUBLANES, (tl + 1) * SUBLANES)
            a = a_ref[blk, sl]
            e = a * e + b_ref[blk, sl]
            p = a * p
            b_ref[blk, sl] = e
            a_ref[blk, sl] = p
        for k in (1, 2, 4):
            keep = rows >= k
            e = e + p * jnp.where(keep, pltpu.roll(e, k, 0), 0.0)
            p = p * jnp.where(keep, pltpu.roll(p, k, 0), 1.0)
        hin = hc_ref[:, sl]
        e = e + p * hin
        init = jnp.where(rows == 0, hin, pltpu.roll(e, 1, 0))
        hc_ref[:, sl] = _bcast_last_row(e)

        hs = (b_ref[:, sl].reshape(steps, SUBLANES, bw)
              + a_ref[:, sl].reshape(steps, SUBLANES, bw) * init[None]).reshape(tm, bw)
        y_ref[:, sl] = (hs * jax.nn.gelu(gate[:, sl])).astype(BF16)
    o_ref[...] = h + _dot(y_ref[...], wout_ref[...])


def _rg_block(h, g, w_in, conv_w, conv_b, w_a, b_a, w_x, b_x, lam, w_out, *, seq):
    t, d = h.shape
    tm = TILE_ROWS
    heads = w_a.shape[0]
    kw = conv_w.shape[0]
    wax = jnp.concatenate([w_a, w_x], axis=-1).astype(BF16)
    row = pl.BlockSpec((tm, d), lambda i: (i, 0))
    args = (g.reshape(1, d), w_in.astype(BF16), conv_w, conv_b.reshape(1, d), wax,
            b_a.reshape(1, d), b_x.reshape(1, d), lam.reshape(1, d), w_out.astype(BF16))
    kern = functools.partial(_rg_kernel, tiles_per_seq=seq // tm, heads=heads)
    return pl.pallas_call(
        kern,
        grid=(t // tm,),
        in_specs=[row] + [_full_spec(a) for a in args],
        out_specs=row,
        out_shape=jax.ShapeDtypeStruct((t, d), F32),
        scratch_shapes=[pltpu.VMEM(((kw - 1) * SUBLANES + tm, d), F32),
                        pltpu.VMEM((tm, d + ROW_PAD), F32),
                        pltpu.VMEM((tm, d + ROW_PAD), F32),
                        pltpu.VMEM((kw - 1, SUBLANES, d), F32),
                        pltpu.VMEM((SUBLANES, d), F32),
                        pltpu.VMEM((tm, d), BF16)],
        compiler_params=pltpu.CompilerParams(dimension_semantics=("arbitrary",),
                                             vmem_limit_bytes=VMEM_LIMIT_BYTES),
        name="rglru_mixer",
    )(h, *args)


def _s5_kernel(h_ref, g_ref, win_ref, bblk_ref, tab_ref, pw_ref, cblk_ref, d_ref, wglu_ref,
               wout_ref, o_ref, u_ref, st_ref, sb_ref, y_ref, hc_ref, *, tiles_per_seq,
               lane_blocks):
    tm, d = h_ref.shape
    half = bblk_ref.shape[2] // 2
    n_lc = half // LANES
    steps = tm // SUBLANES
    i = pl.program_id(0)

    @pl.when(i % tiles_per_seq == 0)
    def _():
        hc_ref[...] = jnp.zeros_like(hc_ref)

    h = h_ref[...]
    hn = _rmsnorm(h, g_ref[...]).astype(BF16)
    u_ref[...] = _dot(hn, win_ref[...])
    row0 = _is_row0((SUBLANES, LANES))
    ns = 2 * half

    def lanes(lc):
        return (slice(lc * LANES, (lc + 1) * LANES),
                slice(half + lc * LANES, half + (lc + 1) * LANES))

    def project_in(j):
        uj = u_ref[:, j * LANES:(j + 1) * LANES].astype(BF16)
        st_ref[j % 2, :, 0:ns] = _dot(uj, bblk_ref[j])

    def pair(blocks):
        return jnp.concatenate(blocks, axis=0).astype(BF16)

    project_in(0)
    for j in range(lane_blocks):
        if j + 1 < lane_blocks:
            project_in(j + 1)
        st = st_ref.at[j % 2]
        sb = sb_ref.at[(j // 2) % 2]
        off = (j % 2) * ns

        for lc in range(n_lc):
            lr, li = lanes(lc)
            ar = jnp.broadcast_to(tab_ref[j, 2 * lc, 0:1, :], (SUBLANES, LANES))
            ai = jnp.broadcast_to(tab_ref[j, 2 * lc + 1, 0:1, :], (SUBLANES, LANES))
            er = ei = None
            for t2 in range(0, steps, 2):
                blk_r, blk_i = [], []
                for tl in (t2, t2 + 1):
                    rows = slice(tl * SUBLANES, (tl + 1) * SUBLANES)
                    if tl == 0:
                        er, ei = st[rows, lr], st[rows, li]
                    else:
                        er, ei = (ar * er - ai * ei + st[rows, lr], ar * ei + ai * er + st[rows, li])
                    blk_r.append(er)
                    blk_i.append(ei)
                rows2 = slice(t2 * SUBLANES, (t2 + 2) * SUBLANES)
                sb[rows2, off + lc * LANES:off + (lc + 1) * LANES] = pair(blk_r)
                sb[rows2, off + half + lc * LANES:off + half + (lc + 1) * LANES] = pair(blk_i)

            for s in range(3):
                k = 1 << s
                pr = pw_ref[j, 2 * s, :, lr]
                pi = pw_ref[j, 2 * s + 1, :, lr]
                sr = pltpu.roll(er, k, 0)
                si = pltpu.roll(ei, k, 0)
                er, ei = er + pr * sr - pi * si, ei + pr * si + pi * sr
            qr = pw_ref[j, 6, :, lr]
            qi = pw_ref[j, 7, :, lr]
            cr = hc_ref[j, :, lr]
            ci = hc_ref[j, :, li]
            er, ei = er + qr * cr - qi * ci, ei + qr * ci + qi * cr
            zr = jnp.where(row0, cr, pltpu.roll(er, 1, 0))
            zi = jnp.where(row0, ci, pltpu.roll(ei, 1, 0))
            hc_ref[j, :, lr] = _bcast_last_row(er)
            hc_ref[j, :, li] = _bcast_last_row(ei)

            zr2 = pair([zr, zr])
            zi2 = pair([zi, zi])
            for t2 in range(0, steps, 2):
                tr2 = pair([tab_ref[j, 2 * lc, pl.ds(tl, SUBLANES, stride=0), :] for tl in (t2, t2 + 1)])
                ti2 = pair([tab_ref[j, 2 * lc + 1, pl.ds(tl, SUBLANES, stride=0), :] for tl in (t2, t2 + 1)])
                rows2 = slice(t2 * SUBLANES, (t2 + 2) * SUBLANES)
                fr = off + lc * LANES
                fi = off + half + lc * LANES
                sb[rows2, fr:fr + LANES] = sb[rows2, fr:fr + LANES] + (tr2 * zr2 - ti2 * zi2)
                sb[rows2, fi:fi + LANES] = sb[rows2, fi:fi + LANES] + (tr2 * zi2 + ti2 * zr2)

        if j % 2 == 1:
            jp = j // 2
            y_ref[:, jp * 2 * LANES:(jp + 1) * 2 * LANES] = _dot(sb[:, 0:2 * ns], cblk_ref[jp])

    y = y_ref[...] + d_ref[...] * u_ref[...]
    gl = _dot(jax.nn.gelu(y).astype(BF16), wglu_ref[...])
    out = (gl[:, :d] * _sigmoid(gl[:, d:])).astype(BF16)
    o_ref[...] = h + _dot(out, wout_ref[...])


def _cmul(x, y):
    return x[0] * y[0] - x[1] * y[1], x[0] * y[1] + x[1] * y[0]


def _cpowers(a, n):
    pr, pi = a
    while pr.shape[1] < n:
        tr, ti = _cmul((pr, pi), (pr[:, -1:], pi[:, -1:]))
        pr = jnp.concatenate([pr, tr], axis=1)
        pi = jnp.concatenate([pi, ti], axis=1)
    return pr[:, :n], pi[:, :n]


def _s5_params(a_re, a_im, log_dt, b_re, b_im, c_re, c_im, chunk):
    g, p = a_re.shape
    gc = b_re.shape[-1]
    gpb = LANES // gc
    nb = g // gpb
    dt = jnp.exp(log_dt)[:, None]
    mag = jnp.exp(a_re * dt)
    abr = mag * jnp.cos(a_im * dt)
    abi = mag * jnp.sin(a_im * dt)
    ur, ui = abr - 1.0, abi
    den = a_re * a_re + a_im * a_im
    wr = (ur * a_re + ui * a_im) / den
    wi = (ui * a_re - ur * a_im) / den
    bbr = wr[..., None] * b_re - wi[..., None] * b_im
    bbi = wr[..., None] * b_im + wi[..., None] * b_re
    eye = jnp.eye(gpb, dtype=F32)
    bb = jnp.stack([bbr, bbi]).reshape(2, nb, gpb, p, gc)
    bblk = jnp.einsum('ab,zjapc->jaczbp', eye, bb).reshape(nb, gpb * gc, 2 * gpb * p)
    cc = jnp.stack([c_re, -c_im]).reshape(2, nb, gpb, gc, p)
    cblk = jnp.einsum('ab,zjacp->jzapbc', eye, cc).reshape(nb, 2 * gpb * p, gpb * gc)
    a1 = (abr.reshape(nb, 1, gpb * p), abi.reshape(nb, 1, gpb * p))
    apow = _cpowers(a1, chunk)
    tab = jnp.stack(apow, axis=1)
    tab = tab.reshape(nb, 2, chunk, gpb * p // LANES, LANES).transpose(0, 3, 1, 2, 4)
    tab = tab.reshape(nb, 2 * gpb * p // LANES, chunk, LANES)
    tab = jnp.pad(tab, ((0, 0), (0, 0), (0, SUBLANES), (0, 0)))
    wpow = _cpowers((apow[0][:, -1:], apow[1][:, -1:]), SUBLANES)
    rows = jnp.arange(SUBLANES)[None, :, None]
    planes = []
    for k in (1, 2, 4):
        for part in range(2):
            planes.append(jnp.where(rows >= k, wpow[part][:, k - 1:k], 0.0))
    for part in range(2):
        planes.append(wpow[part])
    pw = jnp.stack(planes, axis=1)
    zeros = jnp.zeros_like(cblk[0::2])
    cblk = jnp.concatenate([jnp.concatenate([cblk[0::2], zeros], axis=2),
                            jnp.concatenate([zeros, cblk[1::2]], axis=2)], axis=1)
    return bblk.astype(BF16), tab, pw, cblk.astype(BF16)


def _s5_block(h, g, w_in, a_re, a_im, log_dt, b_re, b_im, c_re, c_im, dvec, w_glu, w_out,
              *, seq):
    t, d = h.shape
    tm = TILE_ROWS
    bblk, tab, pw, cblk = _s5_params(a_re, a_im, log_dt, b_re, b_im, c_re, c_im, tm // SUBLANES)
    nb = bblk.shape[0]
    ns = bblk.shape[2]
    row = pl.BlockSpec((tm, d), lambda i: (i, 0))
    args = (g.reshape(1, d), w_in.astype(BF16), bblk, tab, pw, cblk, dvec.reshape(1, d),
            w_glu.astype(BF16), w_out.astype(BF16))
    kern = functools.partial(_s5_kernel, tiles_per_seq=seq // tm, lane_blocks=nb)
    return pl.pallas_call(
        kern,
        grid=(t // tm,),
        in_specs=[row] + [_full_spec(a) for a in args],
        out_specs=row,
        out_shape=jax.ShapeDtypeStruct((t, d), F32),
        scratch_shapes=[pltpu.VMEM((tm, d), F32),
                        pltpu.VMEM((2, tm, ns + ROW_PAD), F32),
                        pltpu.VMEM((2, tm, 2 * ns + ROW_PAD), BF16),
                        pltpu.VMEM((tm, d), F32),
                        pltpu.VMEM((nb, SUBLANES, ns), F32)],
        compiler_params=pltpu.CompilerParams(dimension_semantics=("arbitrary",),
                                             vmem_limit_bytes=VMEM_LIMIT_BYTES),
        name="s5_mixer",
    )(h, *args)


def kernel(x, norm_mix_g, norm_ffn_g, norm_final_g, rg_w_in, rg_conv_w, rg_conv_b, rg_w_a, rg_b_a, rg_w_x, rg_b_x, rg_lambda, rg_w_out, s5_w_in, s5_a_re, s5_a_im, s5_log_dt, s5_b_re, s5_b_im, s5_c_re, s5_c_im, s5_d, s5_w_glu, s5_w_out, ffn_w_up, ffn_conv_w, ffn_conv_b, ffn_w_down):
    bsz, seq, d = x.shape
    depth = norm_mix_g.shape[0]
    assert seq % TILE_ROWS == 0
    h = _to_tile_order(x.reshape(bsz * seq, d), TILE_ROWS)
    for i in range(depth):
        j = i // 2
        if i % 2 == 0:
            h = _rg_block(h, norm_mix_g[i], rg_w_in[j], rg_conv_w[j], rg_conv_b[j], rg_w_a[j],
                          rg_b_a[j].reshape(-1), rg_w_x[j], rg_b_x[j].reshape(-1), rg_lambda[j],
                          rg_w_out[j], seq=seq)
        else:
            h = _s5_block(h, norm_mix_g[i], s5_w_in[j], s5_a_re[j], s5_a_im[j], s5_log_dt[j],
                          s5_b_re[j], s5_b_im[j], s5_c_re[j], s5_c_im[j], s5_d[j], s5_w_glu[j],
                          s5_w_out[j], seq=seq)
        h = _ffn_block(h, norm_ffn_g[i], ffn_w_up[i], ffn_conv_w[i], ffn_conv_b[i],
                       ffn_w_down[i], norm_final_g, seq=seq, final_norm=(i == depth - 1))
    return _from_tile_order(h, TILE_ROWS).reshape(bsz, seq, d)
```

```python
import functools

import jax
import jax.numpy as jnp
from jax import lax
from jax.experimental import pallas as pl
from jax.experimental.pallas import tpu as pltpu

NORM_EPS = 1e-6
RG_C = 8.0
SUBLANES = 8
LANES = 128
VMEM_LIMIT_BYTES = 56 * 1024 * 1024
TILE_ROWS = 512
FFN_CHUNK = 512
ROW_PAD = LANES

F32 = jnp.float32
BF16 = jnp.bfloat16


def _rmsnorm(x, g):
    var = jnp.mean(x * x, axis=-1, keepdims=True)
    return x * lax.rsqrt(var + NORM_EPS) * g


def _sigmoid(x):
    return 0.5 * jnp.tanh(0.5 * x) + 0.5


def _dot(a, b):
    return jnp.dot(a, b, preferred_element_type=F32)


def _is_row0(shape):
    return lax.broadcasted_iota(jnp.int32, shape, 0) == 0


def _prev_chunk_rows(cur, prev_tile):
    return jnp.where(_is_row0(cur.shape), pltpu.roll(prev_tile, 1, 0), pltpu.roll(cur, 1, 0))


def _bcast_last_row(x):
    return jnp.broadcast_to(x[SUBLANES - 1:SUBLANES, :], x.shape)


def _full_spec(a):
    nd = a.ndim
    return pl.BlockSpec(a.shape, lambda i: (0,) * nd, pipeline_mode=pl.Buffered(1))


def _to_tile_order(x2d, tm):
    t, d = x2d.shape
    return x2d.reshape(t // tm, SUBLANES, tm // SUBLANES, d).transpose(0, 2, 1, 3).reshape(t, d)


def _from_tile_order(x2d, tm):
    t, d = x2d.shape
    return x2d.reshape(t // tm, tm // SUBLANES, SUBLANES, d).transpose(0, 2, 1, 3).reshape(t, d)


def _ffn_kernel(h_ref, g_ref, wa_ref, wb_ref, cwa_ref, cwb_ref, cba_ref, cbb_ref,
                wd_ref, gf_ref, o_ref, xa_ref, xb_ref, ca_ref, cb_ref, act_ref,
                *, tiles_per_seq, n_chunks, final_norm):
    tm = h_ref.shape[0]
    kw = cwa_ref.shape[1]
    halo = (kw - 1) * SUBLANES
    i = pl.program_id(0)

    @pl.when(i % tiles_per_seq == 0)
    def _():
        ca_ref[...] = jnp.zeros_like(ca_ref)
        cb_ref[...] = jnp.zeros_like(cb_ref)

    h = h_ref[...]
    hn = _rmsnorm(h, g_ref[...]).astype(BF16)

    def conv(x_ref, carry_ref, c, up, cw, cb):
        x_ref[c, halo:halo + tm, :] = up
        out = cb + cw[kw - 1:kw, :] * up
        for s in range(1, kw):
            cur = up[tm - s * SUBLANES:tm - (s - 1) * SUBLANES, :]
            x_ref[c, halo - s * SUBLANES:halo - (s - 1) * SUBLANES, :] = (
                _prev_chunk_rows(cur, carry_ref[c, s - 1]))
            carry_ref[c, s - 1] = cur
        for s in range(1, kw):
            out = out + cw[kw - 1 - s:kw - s, :] * x_ref[c, halo - s * SUBLANES:halo - s * SUBLANES + tm, :]
        return out

    fc = wa_ref.shape[2]
    for c in range(n_chunks):
        ua = _dot(hn, wa_ref[c])
        ub = _dot(hn, wb_ref[c])
        va = conv(xa_ref, ca_ref, c, ua, cwa_ref[c], cba_ref[c])
        vb = conv(xb_ref, cb_ref, c, ub, cwb_ref[c], cbb_ref[c])
        act_ref[:, c * fc:(c + 1) * fc] = (jax.nn.gelu(va) * vb).astype(BF16)
    acc = h + _dot(act_ref[...], wd_ref[...])
    if final_norm:
        acc = _rmsnorm(acc, gf_ref[...])
    o_ref[...] = acc


def _ffn_block(h, g, w_up, conv_w, conv_b, w_down, g_final, *, seq, final_norm):
    t, d = h.shape
    tm, fc = TILE_ROWS, FFN_CHUNK
    dff = w_down.shape[0]
    nc = dff // fc
    kw = conv_w.shape[0]
    wa = w_up[:, :dff].reshape(d, nc, fc).transpose(1, 0, 2).astype(BF16)
    wb = w_up[:, dff:].reshape(d, nc, fc).transpose(1, 0, 2).astype(BF16)
    cwa = conv_w[:, :dff].reshape(kw, nc, fc).transpose(1, 0, 2)
    cwb = conv_w[:, dff:].reshape(kw, nc, fc).transpose(1, 0, 2)
    cba = conv_b[:dff].reshape(nc, 1, fc)
    cbb = conv_b[dff:].reshape(nc, 1, fc)
    wd = w_down.astype(BF16)

    row = pl.BlockSpec((tm, d), lambda i: (i, 0))
    args = (g.reshape(1, d), wa, wb, cwa, cwb, cba, cbb, wd, g_final.reshape(1, d))
    kern = functools.partial(_ffn_kernel, tiles_per_seq=seq // tm, n_chunks=nc,
                             final_norm=final_norm)
    halo = (kw - 1) * SUBLANES
    return pl.pallas_call(
        kern,
        grid=(t // tm,),
        in_specs=[row] + [_full_spec(a) for a in args],
        out_specs=row,
        out_shape=jax.ShapeDtypeStruct((t, d), F32),
        scratch_shapes=[pltpu.VMEM((nc, halo + tm, fc), F32),
                        pltpu.VMEM((nc, halo + tm, fc), F32),
                        pltpu.VMEM((nc, kw - 1, SUBLANES, fc), F32),
                        pltpu.VMEM((nc, kw - 1, SUBLANES, fc), F32),
                        pltpu.VMEM((tm, dff), BF16)],
        compiler_params=pltpu.CompilerParams(dimension_semantics=("arbitrary",),
                                             vmem_limit_bytes=VMEM_LIMIT_BYTES),
        name="conv_ffn",
    )(h, *args)


def _rg_kernel(h_ref, g_ref, win_ref, cw_ref, cb_ref, wax_ref, ba_ref, bx_ref, lam_ref,
               wout_ref, o_ref, xr_ref, a_ref, b_ref, xc_ref, hc_ref, y_ref, *, tiles_per_seq,
               heads):
    tm, d = h_ref.shape
    bw = d // heads
    kw = cw_ref.shape[0]
    halo = (kw - 1) * SUBLANES
    steps = tm // SUBLANES
    i = pl.program_id(0)

    @pl.when(i % tiles_per_seq == 0)
    def _():
        xc_ref[...] = jnp.zeros_like(xc_ref)
        hc_ref[...] = jnp.zeros_like(hc_ref)

    h = h_ref[...]
    hn = _rmsnorm(h, g_ref[...]).astype(BF16)
    xg = _dot(hn, win_ref[...])
    gate = xg[:, d:]
    xp = xg[:, :d]
    xr_ref[halo:halo + tm, :] = xp
    xr = cb_ref[...] + cw_ref[kw - 1:kw, :] * xp
    for s in range(1, kw):
        cur = xp[tm - s * SUBLANES:tm - (s - 1) * SUBLANES, :]
        xr_ref[halo - s * SUBLANES:halo - (s - 1) * SUBLANES, :] = _prev_chunk_rows(cur, xc_ref[s - 1])
        xc_ref[s - 1] = cur
    for s in range(1, kw):
        xr = xr + cw_ref[kw - 1 - s:kw - s, :] * xr_ref[halo - s * SUBLANES:halo - s * SUBLANES + tm, :]

    neg_c_sp = -RG_C * jax.nn.softplus(-lam_ref[...])
    rows = lax.broadcasted_iota(jnp.int32, (SUBLANES, bw), 0)
    for hd in range(heads):
        sl = slice(hd * bw, (hd + 1) * bw)
        xh = xr[:, sl]
        ax = _dot(xh.astype(BF16), wax_ref[hd])
        r = _sigmoid(ax[:, :bw] + ba_ref[:, sl])
        ig = _sigmoid(ax[:, bw:] + bx_ref[:, sl])
        log_a = r * neg_c_sp[:, sl]
        a_ref[:, sl] = jnp.exp(log_a)
        th = jnp.tanh(log_a)
        num = -2.0 * th
        mult = jnp.where(num > 0.0, num * lax.rsqrt(num * (1.0 - th)), 0.0)
        b_ref[:, sl] = mult * (ig * xh)

        e = b_ref[0:SUBLANES, sl]
        p = a_ref[0:SUBLANES, sl]
        for tl in range(1, steps):
            blk = slice(tl * SUBLANES, (tl + 1) * SUBLANES)
            a = a_ref[blk, sl]
            e = a * e + b_ref[blk, sl]
            p = a * p
            b_ref[blk, sl] = e
            a_ref[blk, sl] = p
        for k in (1, 2, 4):
            keep = rows >= k
            e = e + p * jnp.where(keep, pltpu.roll(e, k, 0), 0.0)
            p = p * jnp.where(keep, pltpu.roll(p, k, 0), 1.0)
        hin = hc_ref[:, sl]
        e = e + p * hin
        init = jnp.where(rows == 0, hin, pltpu.roll(e, 1, 0))
        hc_ref[:, sl] = _bcast_last_row(e)

        hs = (b_ref[:, sl].reshape(steps, SUBLANES, bw)
              + a_ref[:, sl].reshape(steps, SUBLANES, bw) * init[None]).reshape(tm, bw)
        y_ref[:, sl] = (hs * jax.nn.gelu(gate[:, sl])).astype(BF16)
    o_ref[...] = h + _dot(y_ref[...], wout_ref[...])


def _rg_block(h, g, w_in, conv_w, conv_b, w_a, b_a, w_x, b_x, lam, w_out, *, seq):
    t, d = h.shape
    tm = TILE_ROWS
    heads = w_a.shape[0]
    kw = conv_w.shape[0]
    wax = jnp.concatenate([w_a, w_x], axis=-1).astype(BF16)
    row = pl.BlockSpec((tm, d), lambda i: (i, 0))
    args = (g.reshape(1, d), w_in.astype(BF16), conv_w, conv_b.reshape(1, d), wax,
            b_a.reshape(1, d), b_x.reshape(1, d), lam.reshape(1, d), w_out.astype(BF16))
    kern = functools.partial(_rg_kernel, tiles_per_seq=seq // tm, heads=heads)
    return pl.pallas_call(
        kern,
        grid=(t // tm,),
        in_specs=[row] + [_full_spec(a) for a in args],
        out_specs=row,
        out_shape=jax.ShapeDtypeStruct((t, d), F32),
        scratch_shapes=[pltpu.VMEM(((kw - 1) * SUBLANES + tm, d), F32),
                        pltpu.VMEM((tm, d + ROW_PAD), F32),
                        pltpu.VMEM((tm, d + ROW_PAD), F32),
                        pltpu.VMEM((kw - 1, SUBLANES, d), F32),
                        pltpu.VMEM((SUBLANES, d), F32),
                        pltpu.VMEM((tm, d), BF16)],
        compiler_params=pltpu.CompilerParams(dimension_semantics=("arbitrary",),
                                             vmem_limit_bytes=VMEM_LIMIT_BYTES),
        name="rglru_mixer",
    )(h, *args)


def _s5_kernel(h_ref, g_ref, win_ref, bblk_ref, tab_ref, pw_ref, cblk_ref, d_ref, wglu_ref,
               wout_ref, o_ref, u_ref, st_ref, sb_ref, y_ref, hc_ref, *, tiles_per_seq,
               lane_blocks):
    tm, d = h_ref.shape
    half = bblk_ref.shape[2] // 2
    n_lc = half // LANES
    steps = tm // SUBLANES
    i = pl.program_id(0)

    @pl.when(i % tiles_per_seq == 0)
    def _():
        hc_ref[...] = jnp.zeros_like(hc_ref)

    h = h_ref[...]
    hn = _rmsnorm(h, g_ref[...]).astype(BF16)
    u_ref[...] = _dot(hn, win_ref[...])
    row0 = _is_row0((SUBLANES, LANES))
    ns = 2 * half

    def lanes(lc):
        return (slice(lc * LANES, (lc + 1) * LANES),
                slice(half + lc * LANES, half + (lc + 1) * LANES))

    def project_in(j):
        uj = u_ref[:, j * LANES:(j + 1) * LANES].astype(BF16)
        st_ref[j % 2, :, 0:ns] = _dot(uj, bblk_ref[j])

    def pair(blocks):
        return jnp.concatenate(blocks, axis=0).astype(BF16)

    project_in(0)
    for j in range(lane_blocks):
        if j + 1 < lane_blocks:
            project_in(j + 1)
        st = st_ref.at[j % 2]
        sb = sb_ref.at[(j // 2) % 2]
        off = (j % 2) * ns

        abar = [(jnp.broadcast_to(tab_ref[j, 2 * lc, 0:1, :], (SUBLANES, LANES)),
                 jnp.broadcast_to(tab_ref[j, 2 * lc + 1, 0:1, :], (SUBLANES, LANES)))
                for lc in range(n_lc)]
        ends = [None] * n_lc
        for t2 in range(0, steps, 2):
            rows2 = slice(t2 * SUBLANES, (t2 + 2) * SUBLANES)
            for lc in range(n_lc):
                lr, li = lanes(lc)
                ar, ai = abar[lc]
                blk_r, blk_i = [], []
                for tl in (t2, t2 + 1):
                    rows = slice(tl * SUBLANES, (tl + 1) * SUBLANES)
                    if tl == 0:
                        er, ei = st[rows, lr], st[rows, li]
                    else:
                        er, ei = ends[lc]
                        er, ei = (ar * er - ai * ei + st[rows, lr], ar * ei + ai * er + st[rows, li])
                    ends[lc] = (er, ei)
                    blk_r.append(er)
                    blk_i.append(ei)
                sb[rows2, off + lc * LANES:off + (lc + 1) * LANES] = pair(blk_r)
                sb[rows2, off + half + lc * LANES:off + half + (lc + 1) * LANES] = pair(blk_i)

        for lc in range(n_lc):
            lr, li = lanes(lc)
            er, ei = ends[lc]
            for s in range(3):
                k = 1 << s
                pr = pw_ref[j, 2 * s, :, lr]
                pi = pw_ref[j, 2 * s + 1, :, lr]
                sr = pltpu.roll(er, k, 0)
                si = pltpu.roll(ei, k, 0)
                er, ei = er + pr * sr - pi * si, ei + pr * si + pi * sr
            qr = pw_ref[j, 6, :, lr]
            qi = pw_ref[j, 7, :, lr]
            cr = hc_ref[j, :, lr]
            ci = hc_ref[j, :, li]
            er, ei = er + qr * cr - qi * ci, ei + qr * ci + qi * cr
            zr = jnp.where(row0, cr, pltpu.roll(er, 1, 0))
            zi = jnp.where(row0, ci, pltpu.roll(ei, 1, 0))
            hc_ref[j, :, lr] = _bcast_last_row(er)
            hc_ref[j, :, li] = _bcast_last_row(ei)

            zr2 = pair([zr, zr])
            zi2 = pair([zi, zi])
            for t2 in range(0, steps, 2):
                tr2 = pair([tab_ref[j, 2 * lc, pl.ds(tl, SUBLANES, stride=0), :] for tl in (t2, t2 + 1)])
                ti2 = pair([tab_ref[j, 2 * lc + 1, pl.ds(tl, SUBLANES, stride=0), :] for tl in (t2, t2 + 1)])
                rows2 = slice(t2 * SUBLANES, (t2 + 2) * SUBLANES)
                fr = off + lc * LANES
                fi = off + half + lc * LANES
                sb[rows2, fr:fr + LANES] = sb[rows2, fr:fr + LANES] + (tr2 * zr2 - ti2 * zi2)
                sb[rows2, fi:fi + LANES] = sb[rows2, fi:fi + LANES] + (tr2 * zi2 + ti2 * zr2)

        if j % 2 == 1:
            jp = j // 2
            y_ref[:, jp * 2 * LANES:(jp + 1) * 2 * LANES] = _dot(sb[:, 0:2 * ns], cblk_ref[jp])

    y = y_ref[...] + d_ref[...] * u_ref[...]
    gl = _dot(jax.nn.gelu(y).astype(BF16), wglu_ref[...])
    out = (gl[:, :d] * _sigmoid(gl[:, d:])).astype(BF16)
    o_ref[...] = h + _dot(out, wout_ref[...])


def _cmul(x, y):
    return x[0] * y[0] - x[1] * y[1], x[0] * y[1] + x[1] * y[0]


def _cpowers(a, n):
    pr, pi = a
    while pr.shape[1] < n:
        tr, ti = _cmul((pr, pi), (pr[:, -1:], pi[:, -1:]))
        pr = jnp.concatenate([pr, tr], axis=1)
        pi = jnp.concatenate([pi, ti], axis=1)
    return pr[:, :n], pi[:, :n]


def _s5_params(a_re, a_im, log_dt, b_re, b_im, c_re, c_im, chunk):
    g, p = a_re.shape
    gc = b_re.shape[-1]
    gpb = LANES // gc
    nb = g // gpb
    dt = jnp.exp(log_dt)[:, None]
    mag = jnp.exp(a_re * dt)
    abr = mag * jnp.cos(a_im * dt)
    abi = mag * jnp.sin(a_im * dt)
    ur, ui = abr - 1.0, abi
    den = a_re * a_re + a_im * a_im
    wr = (ur * a_re + ui * a_im) / den
    wi = (ui * a_re - ur * a_im) / den
    bbr = wr[..., None] * b_re - wi[..., None] * b_im
    bbi = wr[..., None] * b_im + wi[..., None] * b_re
    eye = jnp.eye(gpb, dtype=F32)
    bb = jnp.stack([bbr, bbi]).reshape(2, nb, gpb, p, gc)
    bblk = jnp.einsum('ab,zjapc->jaczbp', eye, bb).reshape(nb, gpb * gc, 2 * gpb * p)
    cc = jnp.stack([c_re, -c_im]).reshape(2, nb, gpb, gc, p)
    cblk = jnp.einsum('ab,zjacp->jzapbc', eye, cc).reshape(nb, 2 * gpb * p, gpb * gc)
    a1 = (abr.reshape(nb, 1, gpb * p), abi.reshape(nb, 1, gpb * p))
    apow = _cpowers(a1, chunk)
    tab = jnp.stack(apow, axis=1)
    tab = tab.reshape(nb, 2, chunk, gpb * p // LANES, LANES).transpose(0, 3, 1, 2, 4)
    tab = tab.reshape(nb, 2 * gpb * p // LANES, chunk, LANES)
    tab = jnp.pad(tab, ((0, 0), (0, 0), (0, SUBLANES), (0, 0)))
    wpow = _cpowers((apow[0][:, -1:], apow[1][:, -1:]), SUBLANES)
    rows = jnp.arange(SUBLANES)[None, :, None]
    planes = []
    for k in (1, 2, 4):
        for part in range(2):
            planes.append(jnp.where(rows >= k, wpow[part][:, k - 1:k], 0.0))
    for part in range(2):
        planes.append(wpow[part])
    pw = jnp.stack(planes, axis=1)
    zeros = jnp.zeros_like(cblk[0::2])
    cblk = jnp.concatenate([jnp.concatenate([cblk[0::2], zeros], axis=2),
                            jnp.concatenate([zeros, cblk[1::2]], axis=2)], axis=1)
    return bblk.astype(BF16), tab, pw, cblk.astype(BF16)


def _s5_block(h, g, w_in, a_re, a_im, log_dt, b_re, b_im, c_re, c_im, dvec, w_glu, w_out,
              *, seq):
    t, d = h.shape
    tm = TILE_ROWS
    bblk, tab, pw, cblk = _s5_params(a_re, a_im, log_dt, b_re, b_im, c_re, c_im, tm // SUBLANES)
    nb = bblk.shape[0]
    ns = bblk.shape[2]
    row = pl.BlockSpec((tm, d), lambda i: (i, 0))
    args = (g.reshape(1, d), w_in.astype(BF16), bblk, tab, pw, cblk, dvec.reshape(1, d),
            w_glu.astype(BF16), w_out.astype(BF16))
    kern = functools.partial(_s5_kernel, tiles_per_seq=seq // tm, lane_blocks=nb)
    return pl.pallas_call(
        kern,
        grid=(t // tm,),
        in_specs=[row] + [_full_spec(a) for a in args],
        out_specs=row,
        out_shape=jax.ShapeDtypeStruct((t, d), F32),
        scratch_shapes=[pltpu.VMEM((tm, d), F32),
                        pltpu.VMEM((2, tm, ns + ROW_PAD), F32),
                        pltpu.VMEM((2, tm, 2 * ns + ROW_PAD), BF16),
                        pltpu.VMEM((tm, d), F32),
                        pltpu.VMEM((nb, SUBLANES, ns), F32)],
        compiler_params=pltpu.CompilerParams(dimension_semantics=("arbitrary",),
                                             vmem_limit_bytes=VMEM_LIMIT_BYTES),
        name="s5_mixer",
    )(h, *args)


def kernel(x, norm_mix_g, norm_ffn_g, norm_final_g, rg_w_in, rg_conv_w, rg_conv_b, rg_w_a, rg_b_a, rg_w_x, rg_b_x, rg_lambda, rg_w_out, s5_w_in, s5_a_re, s5_a_im, s5_log_dt, s5_b_re, s5_b_im, s5_c_re, s5_c_im, s5_d, s5_w_glu, s5_w_out, ffn_w_up, ffn_conv_w, ffn_conv_b, ffn_w_down):
    bsz, seq, d = x.shape
    depth = norm_mix_g.shape[0]
    assert seq % TILE_ROWS == 0
    h = _to_tile_order(x.reshape(bsz * seq, d), TILE_ROWS)
    for i in range(depth):
        j = i // 2
        if i % 2 == 0:
            h = _rg_block(h, norm_mix_g[i], rg_w_in[j], rg_conv_w[j], rg_conv_b[j], rg_w_a[j],
                          rg_b_a[j].reshape(-1), rg_w_x[j], rg_b_x[j].reshape(-1), rg_lambda[j],
                          rg_w_out[j], seq=seq)
        else:
            h = _s5_block(h, norm_mix_g[i], s5_w_in[j], s5_a_re[j], s5_a_im[j], s5_log_dt[j],
                          s5_b_re[j], s5_b_im[j], s5_c_re[j], s5_c_im[j], s5_d[j], s5_w_glu[j],
                          s5_w_out[j], seq=seq)
        h = _ffn_block(h, norm_ffn_g[i], ffn_w_up[i], ffn_conv_w[i], ffn_conv_b[i],
                       ffn_w_down[i], norm_final_g, seq=seq, final_norm=(i == depth - 1))
    return _from_tile_order(h, TILE_ROWS).reshape(bsz, seq, d)
```

```python
import functools

import jax
import jax.numpy as jnp
from jax import lax
from jax.experimental import pallas as pl
from jax.experimental.pallas import tpu as pltpu

NORM_EPS = 1e-6
RG_C = 8.0
SUBLANES = 8
LANES = 128
VMEM_LIMIT_BYTES = 56 * 1024 * 1024
TILE_ROWS = 512
FFN_CHUNK = 512
ROW_PAD = LANES

F32 = jnp.float32
BF16 = jnp.bfloat16


def _rmsnorm(x, g):
    var = jnp.mean(x * x, axis=-1, keepdims=True)
    return x * lax.rsqrt(var + NORM_EPS) * g


def _sigmoid(x):
    return 0.5 * jnp.tanh(0.5 * x) + 0.5


def _dot(a, b):
    return jnp.dot(a, b, preferred_element_type=F32)


def _is_row0(shape):
    return lax.broadcasted_iota(jnp.int32, shape, 0) == 0


def _prev_chunk_rows(cur, prev_tile):
    return jnp.where(_is_row0(cur.shape), pltpu.roll(prev_tile, 1, 0), pltpu.roll(cur, 1, 0))


def _bcast_last_row(x):
    return jnp.broadcast_to(x[SUBLANES - 1:SUBLANES, :], x.shape)


def _full_spec(a):
    nd = a.ndim
    return pl.BlockSpec(a.shape, lambda i: (0,) * nd, pipeline_mode=pl.Buffered(1))


def _to_tile_order(x2d, tm):
    t, d = x2d.shape
    return x2d.reshape(t // tm, SUBLANES, tm // SUBLANES, d).transpose(0, 2, 1, 3).reshape(t, d)


def _from_tile_order(x2d, tm):
    t, d = x2d.shape
    return x2d.reshape(t // tm, tm // SUBLANES, SUBLANES, d).transpose(0, 2, 1, 3).reshape(t, d)


def _ffn_kernel(h_ref, g_ref, wa_ref, wb_ref, cwa_ref, cwb_ref, cba_ref, cbb_ref,
                wd_ref, gf_ref, o_ref, xa_ref, xb_ref, ca_ref, cb_ref, act_ref,
                *, tiles_per_seq, n_chunks, final_norm):
    tm = h_ref.shape[0]
    kw = cwa_ref.shape[1]
    halo = (kw - 1) * SUBLANES
    i = pl.program_id(0)

    @pl.when(i % tiles_per_seq == 0)
    def _():
        ca_ref[...] = jnp.zeros_like(ca_ref)
        cb_ref[...] = jnp.zeros_like(cb_ref)

    h = h_ref[...]
    hn = _rmsnorm(h, g_ref[...]).astype(BF16)

    def conv(x_ref, carry_ref, c, up, cw, cb):
        x_ref[c, halo:halo + tm, :] = up
        out = cb + cw[kw - 1:kw, :] * up
        for s in range(1, kw):
            cur = up[tm - s * SUBLANES:tm - (s - 1) * SUBLANES, :]
            x_ref[c, halo - s * SUBLANES:halo - (s - 1) * SUBLANES, :] = (
                _prev_chunk_rows(cur, carry_ref[c, s - 1]))
            carry_ref[c, s - 1] = cur
        for s in range(1, kw):
            out = out + cw[kw - 1 - s:kw - s, :] * x_ref[c, halo - s * SUBLANES:halo - s * SUBLANES + tm, :]
        return out

    fc = wa_ref.shape[2]
    for c in range(n_chunks):
        ua = _dot(hn, wa_ref[c])
        ub = _dot(hn, wb_ref[c])
        va = conv(xa_ref, ca_ref, c, ua, cwa_ref[c], cba_ref[c])
        vb = conv(xb_ref, cb_ref, c, ub, cwb_ref[c], cbb_ref[c])
        act_ref[:, c * fc:(c + 1) * fc] = (jax.nn.gelu(va) * vb).astype(BF16)
    acc = h + _dot(act_ref[...], wd_ref[...])
    if final_norm:
        acc = _rmsnorm(acc, gf_ref[...])
    o_ref[...] = acc


def _ffn_block(h, g, w_up, conv_w, conv_b, w_down, g_final, *, seq, final_norm):
    t, d = h.shape
    tm, fc = TILE_ROWS, FFN_CHUNK
    dff = w_down.shape[0]
    nc = dff // fc
    kw = conv_w.shape[0]
    wa = w_up[:, :dff].reshape(d, nc, fc).transpose(1, 0, 2).astype(BF16)
    wb = w_up[:, dff:].reshape(d, nc, fc).transpose(1, 0, 2).astype(BF16)
    cwa = conv_w[:, :dff].reshape(kw, nc, fc).transpose(1, 0, 2)
    cwb = conv_w[:, dff:].reshape(kw, nc, fc).transpose(1, 0, 2)
    cba = conv_b[:dff].reshape(nc, 1, fc)
    cbb = conv_b[dff:].reshape(nc, 1, fc)
    wd = w_down.astype(BF16)

    row = pl.BlockSpec((tm, d), lambda i: (i, 0))
    args = (g.reshape(1, d), wa, wb, cwa, cwb, cba, cbb, wd, g_final.reshape(1, d))
    kern = functools.partial(_ffn_kernel, tiles_per_seq=seq // tm, n_chunks=nc,
                             final_norm=final_norm)
    halo = (kw - 1) * SUBLANES
    return pl.pallas_call(
        kern,
        grid=(t // tm,),
        in_specs=[row] + [_full_spec(a) for a in args],
        out_specs=row,
        out_shape=jax.ShapeDtypeStruct((t, d), F32),
        scratch_shapes=[pltpu.VMEM((nc, halo + tm, fc), F32),
                        pltpu.VMEM((nc, halo + tm, fc), F32),
                        pltpu.VMEM((nc, kw - 1, SUBLANES, fc), F32),
                        pltpu.VMEM((nc, kw - 1, SUBLANES, fc), F32),
                        pltpu.VMEM((tm, dff), BF16)],
        compiler_params=pltpu.CompilerParams(dimension_semantics=("arbitrary",),
                                             vmem_limit_bytes=VMEM_LIMIT_BYTES),
        name="conv_ffn",
    )(h, *args)


def _rg_kernel(h_ref, g_ref, win_ref, cw_ref, cb_ref, wax_ref, ba_ref, bx_ref, lam_ref,
               wout_ref, o_ref, xr_ref, a_ref, b_ref, xc_ref, hc_ref, y_ref, *, tiles_per_seq,
               heads):
    tm, d = h_ref.shape
    bw = d // heads
    kw = cw_ref.shape[0]
    halo = (kw - 1) * SUBLANES
    steps = tm // SUBLANES
    i = pl.program_id(0)

    @pl.when(i % tiles_per_seq == 0)
    def _():
        xc_ref[...] = jnp.zeros_like(xc_ref)
        hc_ref[...] = jnp.zeros_like(hc_ref)

    h = h_ref[...]
    hn = _rmsnorm(h, g_ref[...]).astype(BF16)
    xg = _dot(hn, win_ref[...])
    gate = xg[:, d:]
    xp = xg[:, :d]
    xr_ref[halo:halo + tm, :] = xp
    xr = cb_ref[...] + cw_ref[kw - 1:kw, :] * xp
    for s in range(1, kw):
        cur = xp[tm - s * SUBLANES:tm - (s - 1) * SUBLANES, :]
        xr_ref[halo - s * SUBLANES:halo - (s - 1) * SUBLANES, :] = _prev_chunk_rows(cur, xc_ref[s - 1])
        xc_ref[s - 1] = cur
    for s in range(1, kw):
        xr = xr + cw_ref[kw - 1 - s:kw - s, :] * xr_ref[halo - s * SUBLANES:halo - s * SUBLANES + tm, :]

    neg_c_sp = -RG_C * jax.nn.softplus(-lam_ref[...])
    rows = lax.broadcasted_iota(jnp.int32, (SUBLANES, bw), 0)
    for hd in range(heads):
        sl = slice(hd * bw, (hd + 1) * bw)
        xh = xr[:, sl]
        ax = _dot(xh.astype(BF16), wax_ref[hd])
        r = _sigmoid(ax[:, :bw] + ba_ref[:, sl])
        ig = _sigmoid(ax[:, bw:] + bx_ref[:, sl])
        log_a = r * neg_c_sp[:, sl]
        a_ref[:, sl] = jnp.exp(log_a)
        th = jnp.tanh(log_a)
        num = -2.0 * th
        mult = jnp.where(num > 0.0, num * lax.rsqrt(num * (1.0 - th)), 0.0)
        b_ref[:, sl] = mult * (ig * xh)

        e = b_ref[0:SUBLANES, sl]
        p = a_ref[0:SUBLANES, sl]
        for tl in range(1, steps):
            blk = slice(tl * SUBLANES, (tl + 1) * SUBLANES)
            a = a_ref[blk, sl]
            e = a * e + b_ref[blk, sl]
            p = a * p
            b_ref[blk, sl] = e
            a_ref[blk, sl] = p
        for k in (1, 2, 4):
            keep = rows >= k
            e = e + p * jnp.where(keep, pltpu.roll(e, k, 0), 0.0)
            p = p * jnp.where(keep, pltpu.roll(p, k, 0), 1.0)
        hin = hc_ref[:, sl]
        e = e + p * hin
        init = jnp.where(rows == 0, hin, pltpu.roll(e, 1, 0))
        hc_ref[:, sl] = _bcast_last_row(e)

        hs = (b_ref[:, sl].reshape(steps, SUBLANES, bw)
              + a_ref[:, sl].reshape(steps, SUBLANES, bw) * init[None]).reshape(tm, bw)
        y_ref[:, sl] = (hs * jax.nn.gelu(gate[:, sl])).astype(BF16)
    o_ref[...] = h + _dot(y_ref[...], wout_ref[...])


def _rg_block(h, g, w_in, conv_w, conv_b, w_a, b_a, w_x, b_x, lam, w_out, *, seq):
    t, d = h.shape
    tm = TILE_ROWS
    heads = w_a.shape[0]
    kw = conv_w.shape[0]
    wax = jnp.concatenate([w_a, w_x], axis=-1).astype(BF16)
    row = pl.BlockSpec((tm, d), lambda i: (i, 0))
    args = (g.reshape(1, d), w_in.astype(BF16), conv_w, conv_b.reshape(1, d), wax,
            b_a.reshape(1, d), b_x.reshape(1, d), lam.reshape(1, d), w_out.astype(BF16))
    kern = functools.partial(_rg_kernel, tiles_per_seq=seq // tm, heads=heads)
    return pl.pallas_call(
        kern,
        grid=(t // tm,),
        in_specs=[row] + [_full_spec(a) for a in args],
        out_specs=row,
        out_shape=jax.ShapeDtypeStruct((t, d), F32),
        scratch_shapes=[pltpu.VMEM(((kw - 1) * SUBLANES + tm, d), F32),
                        pltpu.VMEM((tm, d + ROW_PAD), F32),
                        pltpu.VMEM((tm, d + ROW_PAD), F32),
                        pltpu.VMEM((kw - 1, SUBLANES, d), F32),
                        pltpu.VMEM((SUBLANES, d), F32),
                        pltpu.VMEM((tm, d), BF16)],
        compiler_params=pltpu.CompilerParams(dimension_semantics=("arbitrary",),
                                             vmem_limit_bytes=VMEM_LIMIT_BYTES),
        name="rglru_mixer",
    )(h, *args)


def _s5_kernel(h_ref, g_ref, win_ref, bblk_ref, tab_ref, pw_ref, cblk_ref, d_ref, wglu_ref,
               wout_ref, o_ref, u_ref, st_ref, sb_ref, y_ref, hc_ref, *, tiles_per_seq,
               lane_blocks):
    tm, d = h_ref.shape
    half = bblk_ref.shape[2] // 2
    n_lc = half // LANES
    steps = tm // SUBLANES
    i = pl.program_id(0)

    @pl.when(i % tiles_per_seq == 0)
    def _():
        hc_ref[...] = jnp.zeros_like(hc_ref)

    h = h_ref[...]
    hn = _rmsnorm(h, g_ref[...]).astype(BF16)
    u_ref[...] = _dot(hn, win_ref[...])
    row0 = _is_row0((SUBLANES, LANES))
    ns = 2 * half

    def lanes(lc):
        return (slice(lc * LANES, (lc + 1) * LANES),
                slice(half + lc * LANES, half + (lc + 1) * LANES))

    def project_in(j):
        uj = u_ref[:, j * LANES:(j + 1) * LANES].astype(BF16)
        st_ref[j % 2, :, 0:ns] = _dot(uj, bblk_ref[j])

    def pair(blocks):
        return jnp.concatenate(blocks, axis=0).astype(BF16)

    project_in(0)
    for j in range(lane_blocks):
        if j + 1 < lane_blocks:
            project_in(j + 1)
        st = st_ref.at[j % 2]
        sb = sb_ref.at[(j // 2) % 2]
        off = (j % 2) * ns

        abar = [(jnp.broadcast_to(tab_ref[j, 2 * lc, 0:1, :], (SUBLANES, LANES)),
                 jnp.broadcast_to(tab_ref[j, 2 * lc + 1, 0:1, :], (SUBLANES, LANES)))
                for lc in range(n_lc)]
        ends = [None] * n_lc
        for t2 in range(0, steps, 2):
            rows2 = slice(t2 * SUBLANES, (t2 + 2) * SUBLANES)
            for lc in range(n_lc):
                lr, li = lanes(lc)
                ar, ai = abar[lc]
                blk_r, blk_i = [], []
                for tl in (t2, t2 + 1):
                    rows = slice(tl * SUBLANES, (tl + 1) * SUBLANES)
                    if tl == 0:
                        er, ei = st[rows, lr], st[rows, li]
                    else:
                        er, ei = ends[lc]
                        er, ei = (ar * er - ai * ei + st[rows, lr], ar * ei + ai * er + st[rows, li])
                    ends[lc] = (er, ei)
                    blk_r.append(er)
                    blk_i.append(ei)
                sb[rows2, off + lc * LANES:off + (lc + 1) * LANES] = pair(blk_r)
                sb[rows2, off + half + lc * LANES:off + half + (lc + 1) * LANES] = pair(blk_i)

        for lc in range(n_lc):
            lr, li = lanes(lc)
            er, ei = ends[lc]
            for s in range(3):
                k = 1 << s
                pr = pw_ref[j, 2 * s, :, lr]
                pi = pw_ref[j, 2 * s + 1, :, lr]
                sr = pltpu.roll(er, k, 0)
                si = pltpu.roll(ei, k, 0)
                er, ei = er + pr * sr - pi * si, ei + pr * si + pi * sr
            qr = pw_ref[j, 6, :, lr]
            qi = pw_ref[j, 7, :, lr]
            cr = hc_ref[j, :, lr]
            ci = hc_ref[j, :, li]
            er, ei = er + qr * cr - qi * ci, ei + qr * ci + qi * cr
            zr = jnp.where(row0, cr, pltpu.roll(er, 1, 0))
            zi = jnp.where(row0, ci, pltpu.roll(ei, 1, 0))
            hc_ref[j, :, lr] = _bcast_last_row(er)
            hc_ref[j, :, li] = _bcast_last_row(ei)

            zr2 = pair([zr, zr])
            zi2 = pair([zi, zi])
            for t2 in range(0, steps, 2):
                tr2 = pair([tab_ref[j, 2 * lc, pl.ds(tl, SUBLANES, stride=0), :] for tl in (t2, t2 + 1)])
                ti2 = pair([tab_ref[j, 2 * lc + 1, pl.ds(tl, SUBLANES, stride=0), :] for tl in (t2, t2 + 1)])
                rows2 = slice(t2 * SUBLANES, (t2 + 2) * SUBLANES)
                fr = off + lc * LANES
                fi = off + half + lc * LANES
                sb[rows2, fr:fr + LANES] = sb[rows2, fr:fr + LANES] + (tr2 * zr2 - ti2 * zi2)
                sb[rows2, fi:fi + LANES] = sb[rows2, fi:fi + LANES] + (tr2 * zi2 + ti2 * zr2)

        if j % 2 == 1:
            jp = j // 2
            y_ref[:, jp * 2 * LANES:(jp + 1) * 2 * LANES] = _dot(sb[:, 0:2 * ns], cblk_ref[jp])

    y = y_ref[...] + d_ref[...] * u_ref[...]
    gl = _dot(jax.nn.gelu(y).astype(BF16), wglu_ref[...])
    out = (gl[:, :d] * _sigmoid(gl[:, d:])).astype(BF16)
    o_ref[...] = h + _dot(out, wout_ref[...])


def _cmul(x, y):
    return x[0] * y[0] - x[1] * y[1], x[0] * y[1] + x[1] * y[0]


def _cpowers(a, n):
    pr, pi = a
    while pr.shape[1] < n:
        tr, ti = _cmul((pr, pi), (pr[:, -1:], pi[:, -1:]))
        pr = jnp.concatenate([pr, tr], axis=1)
        pi = jnp.concatenate([pi, ti], axis=1)
    return pr[:, :n], pi[:, :n]


def _s5_params(a_re, a_im, log_dt, b_re, b_im, c_re, c_im, chunk):
    g, p = a_re.shape
    gc = b_re.shape[-1]
    gpb = LANES // gc
    nb = g // gpb
    dt = jnp.exp(log_dt)[:, None]
    mag = jnp.exp(a_re * dt)
    abr = mag * jnp.cos(a_im * dt)
    abi = mag * jnp.sin(a_im * dt)
    ur, ui = abr - 1.0, abi
    den = a_re * a_re + a_im * a_im
    wr = (ur * a_re + ui * a_im) / den
    wi = (ui * a_re - ur * a_im) / den
    bbr = wr[..., None] * b_re - wi[..., None] * b_im
    bbi = wr[..., None] * b_im + wi[..., None] * b_re
    eye = jnp.eye(gpb, dtype=F32)
    bb = jnp.stack([bbr, bbi]).reshape(2, nb, gpb, p, gc)
    bblk = jnp.einsum('ab,zjapc->jaczbp', eye, bb).reshape(nb, gpb * gc, 2 * gpb * p)
    cc = jnp.stack([c_re, -c_im]).reshape(2, nb, gpb, gc, p)
    cblk = jnp.einsum('ab,zjacp->jzapbc', eye, cc).reshape(nb, 2 * gpb * p, gpb * gc)
    a1 = (abr.reshape(nb, 1, gpb * p), abi.reshape(nb, 1, gpb * p))
    apow = _cpowers(a1, chunk)
    tab = jnp.stack(apow, axis=1)
    tab = tab.reshape(nb, 2, chunk, gpb * p // LANES, LANES).transpose(0, 3, 1, 2, 4)
    tab = tab.reshape(nb, 2 * gpb * p // LANES, chunk, LANES)
    tab = jnp.pad(tab, ((0, 0), (0, 0), (0, SUBLANES), (0, 0)))
    wpow = _cpowers((apow[0][:, -1:], apow[1][:, -1:]), SUBLANES)
    rows = jnp.arange(SUBLANES)[None, :, None]
    planes = []
    for k in (1, 2, 4):
        for part in range(2):
            planes.append(jnp.where(rows >= k, wpow[part][:, k - 1:k], 0.0))
    for part in range(2):
        planes.append(wpow[part])
    pw = jnp.stack(planes, axis=1)
    zeros = jnp.zeros_like(cblk[0::2])
    cblk = jnp.concatenate([jnp.concatenate([cblk[0::2], zeros], axis=2),
                            jnp.concatenate([zeros, cblk[1::2]], axis=2)], axis=1)
    return bblk.astype(BF16), tab, pw, cblk.astype(BF16)


def _s5_block(h, g, w_in, params, dvec, w_glu, w_out, *, seq):
    t, d = h.shape
    tm = TILE_ROWS
    bblk, tab, pw, cblk = params
    nb = bblk.shape[0]
    ns = bblk.shape[2]
    row = pl.BlockSpec((tm, d), lambda i: (i, 0))
    args = (g.reshape(1, d), w_in.astype(BF16), bblk, tab, pw, cblk, dvec.reshape(1, d),
            w_glu.astype(BF16), w_out.astype(BF16))
    kern = functools.partial(_s5_kernel, tiles_per_seq=seq // tm, lane_blocks=nb)
    return pl.pallas_call(
        kern,
        grid=(t // tm,),
        in_specs=[row] + [_full_spec(a) for a in args],
        out_specs=row,
        out_shape=jax.ShapeDtypeStruct((t, d), F32),
        scratch_shapes=[pltpu.VMEM((tm, d), F32),
                        pltpu.VMEM((2, tm, ns + ROW_PAD), F32),
                        pltpu.VMEM((2, tm, 2 * ns + ROW_PAD), BF16),
                        pltpu.VMEM((tm, d), F32),
                        pltpu.VMEM((nb, SUBLANES, ns), F32)],
        compiler_params=pltpu.CompilerParams(dimension_semantics=("arbitrary",),
                                             vmem_limit_bytes=VMEM_LIMIT_BYTES),
        name="s5_mixer",
    )(h, *args)


def kernel(x, norm_mix_g, norm_ffn_g, norm_final_g, rg_w_in, rg_conv_w, rg_conv_b, rg_w_a, rg_b_a, rg_w_x, rg_b_x, rg_lambda, rg_w_out, s5_w_in, s5_a_re, s5_a_im, s5_log_dt, s5_b_re, s5_b_im, s5_c_re, s5_c_im, s5_d, s5_w_glu, s5_w_out, ffn_w_up, ffn_conv_w, ffn_conv_b, ffn_w_down):
    bsz, seq, d = x.shape
    depth = norm_mix_g.shape[0]
    assert seq % TILE_ROWS == 0
    h = _to_tile_order(x.reshape(bsz * seq, d), TILE_ROWS)
    s5_params = jax.vmap(functools.partial(_s5_params, chunk=TILE_ROWS // SUBLANES))(
        s5_a_re, s5_a_im, s5_log_dt, s5_b_re, s5_b_im, s5_c_re, s5_c_im)
    for i in range(depth):
        j = i // 2
        if i % 2 == 0:
            h = _rg_block(h, norm_mix_g[i], rg_w_in[j], rg_conv_w[j], rg_conv_b[j], rg_w_a[j],
                          rg_b_a[j].reshape(-1), rg_w_x[j], rg_b_x[j].reshape(-1), rg_lambda[j],
                          rg_w_out[j], seq=seq)
        else:
            h = _s5_block(h, norm_mix_g[i], s5_w_in[j], [p[j] for p in s5_params], s5_d[j],
                          s5_w_glu[j], s5_w_out[j], seq=seq)
        h = _ffn_block(h, norm_ffn_g[i], ffn_w_up[i], ffn_conv_w[i], ffn_conv_b[i],
                       ffn_w_down[i], norm_final_g, seq=seq, final_norm=(i == depth - 1))
    return _from_tile_order(h, TILE_ROWS).reshape(bsz, seq, d)
```

```python
import functools

import jax
import jax.numpy as jnp
from jax import lax
from jax.experimental import pallas as pl
from jax.experimental.pallas import tpu as pltpu

NORM_EPS = 1e-6
RG_C = 8.0
SUBLANES = 8
LANES = 128
VMEM_LIMIT_BYTES = 56 * 1024 * 1024
TILE_ROWS = 512
FFN_CHUNK = 512
ROW_PAD = LANES

F32 = jnp.float32
BF16 = jnp.bfloat16


def _rmsnorm(x, g):
    var = jnp.mean(x * x, axis=-1, keepdims=True)
    return x * lax.rsqrt(var + NORM_EPS) * g


def _sigmoid(x):
    return 0.5 * jnp.tanh(0.5 * x) + 0.5


def _dot(a, b):
    return jnp.dot(a, b, preferred_element_type=F32)


def _is_row0(shape):
    return lax.broadcasted_iota(jnp.int32, shape, 0) == 0


def _prev_chunk_rows(cur, prev_tile):
    return jnp.where(_is_row0(cur.shape), pltpu.roll(prev_tile, 1, 0), pltpu.roll(cur, 1, 0))


def _bcast_last_row(x):
    return jnp.broadcast_to(x[SUBLANES - 1:SUBLANES, :], x.shape)


def _full_spec(a):
    nd = a.ndim
    return pl.BlockSpec(a.shape, lambda i: (0,) * nd, pipeline_mode=pl.Buffered(1))


def _to_tile_order(x2d, tm):
    t, d = x2d.shape
    return x2d.reshape(t // tm, SUBLANES, tm // SUBLANES, d).transpose(0, 2, 1, 3).reshape(t, d)


def _ffn_kernel(h_ref, g_ref, wa_ref, wb_ref, cwa_ref, cwb_ref, cba_ref, cbb_ref,
                wd_ref, gf_ref, o_ref, xa_ref, xb_ref, ca_ref, cb_ref, act_ref, un_ref,
                *, tiles_per_seq, n_chunks, final_norm):
    tm = h_ref.shape[0]
    kw = cwa_ref.shape[1]
    halo = (kw - 1) * SUBLANES
    i = pl.program_id(0)

    @pl.when(i % tiles_per_seq == 0)
    def _():
        ca_ref[...] = jnp.zeros_like(ca_ref)
        cb_ref[...] = jnp.zeros_like(cb_ref)

    h = h_ref[...]
    hn = _rmsnorm(h, g_ref[...]).astype(BF16)

    def conv(x_ref, carry_ref, c, up, cw, cb):
        x_ref[c, halo:halo + tm, :] = up
        out = cb + cw[kw - 1:kw, :] * up
        for s in range(1, kw):
            cur = up[tm - s * SUBLANES:tm - (s - 1) * SUBLANES, :]
            x_ref[c, halo - s * SUBLANES:halo - (s - 1) * SUBLANES, :] = (
                _prev_chunk_rows(cur, carry_ref[c, s - 1]))
            carry_ref[c, s - 1] = cur
        for s in range(1, kw):
            out = out + cw[kw - 1 - s:kw - s, :] * x_ref[c, halo - s * SUBLANES:halo - s * SUBLANES + tm, :]
        return out

    fc = wa_ref.shape[2]
    for c in range(n_chunks):
        ua = _dot(hn, wa_ref[c])
        ub = _dot(hn, wb_ref[c])
        va = conv(xa_ref, ca_ref, c, ua, cwa_ref[c], cba_ref[c])
        vb = conv(xb_ref, cb_ref, c, ub, cwb_ref[c], cbb_ref[c])
        act_ref[:, c * fc:(c + 1) * fc] = (jax.nn.gelu(va) * vb).astype(BF16)
    acc = h + _dot(act_ref[...], wd_ref[...])
    if final_norm:
        acc = _rmsnorm(acc, gf_ref[...])
        d = acc.shape[1]
        steps = tm // SUBLANES
        for c in range(d // LANES):
            un_ref[c] = acc[:, c * LANES:(c + 1) * LANES]
        for q in range(SUBLANES):
            for c in range(d // LANES):
                o_ref[q * steps:(q + 1) * steps, c * LANES:(c + 1) * LANES] = (
                    un_ref[c, pl.ds(q, steps, stride=SUBLANES), :])
    else:
        o_ref[...] = acc


def _ffn_block(h, g, w_up, conv_w, conv_b, w_down, g_final, *, seq, final_norm):
    t, d = h.shape
    tm, fc = TILE_ROWS, FFN_CHUNK
    dff = w_down.shape[0]
    nc = dff // fc
    kw = conv_w.shape[0]
    wa = w_up[:, :dff].reshape(d, nc, fc).transpose(1, 0, 2).astype(BF16)
    wb = w_up[:, dff:].reshape(d, nc, fc).transpose(1, 0, 2).astype(BF16)
    cwa = conv_w[:, :dff].reshape(kw, nc, fc).transpose(1, 0, 2)
    cwb = conv_w[:, dff:].reshape(kw, nc, fc).transpose(1, 0, 2)
    cba = conv_b[:dff].reshape(nc, 1, fc)
    cbb = conv_b[dff:].reshape(nc, 1, fc)
    wd = w_down.astype(BF16)

    row = pl.BlockSpec((tm, d), lambda i: (i, 0))
    args = (g.reshape(1, d), wa, wb, cwa, cwb, cba, cbb, wd, g_final.reshape(1, d))
    kern = functools.partial(_ffn_kernel, tiles_per_seq=seq // tm, n_chunks=nc,
                             final_norm=final_norm)
    halo = (kw - 1) * SUBLANES
    return pl.pallas_call(
        kern,
        grid=(t // tm,),
        in_specs=[row] + [_full_spec(a) for a in args],
        out_specs=row,
        out_shape=jax.ShapeDtypeStruct((t, d), F32),
        scratch_shapes=[pltpu.VMEM((nc, halo + tm, fc), F32),
                        pltpu.VMEM((nc, halo + tm, fc), F32),
                        pltpu.VMEM((nc, kw - 1, SUBLANES, fc), F32),
                        pltpu.VMEM((nc, kw - 1, SUBLANES, fc), F32),
                        pltpu.VMEM((tm, dff), BF16),
                        pltpu.VMEM((d // LANES, tm, LANES), F32)],
        compiler_params=pltpu.CompilerParams(dimension_semantics=("arbitrary",),
                                             vmem_limit_bytes=VMEM_LIMIT_BYTES),
        name="conv_ffn",
    )(h, *args)


def _rg_kernel(h_ref, g_ref, win_ref, cw_ref, cb_ref, wax_ref, ba_ref, bx_ref, lam_ref,
               wout_ref, o_ref, xr_ref, a_ref, b_ref, xc_ref, hc_ref, y_ref, *, tiles_per_seq,
               heads):
    tm, d = h_ref.shape
    bw = d // heads
    kw = cw_ref.shape[0]
    halo = (kw - 1) * SUBLANES
    steps = tm // SUBLANES
    i = pl.program_id(0)

    @pl.when(i % tiles_per_seq == 0)
    def _():
        xc_ref[...] = jnp.zeros_like(xc_ref)
        hc_ref[...] = jnp.zeros_like(hc_ref)

    h = h_ref[...]
    hn = _rmsnorm(h, g_ref[...]).astype(BF16)
    xg = _dot(hn, win_ref[...])
    gate = xg[:, d:]
    xp = xg[:, :d]
    xr_ref[halo:halo + tm, :] = xp
    xr = cb_ref[...] + cw_ref[kw - 1:kw, :] * xp
    for s in range(1, kw):
        cur = xp[tm - s * SUBLANES:tm - (s - 1) * SUBLANES, :]
        xr_ref[halo - s * SUBLANES:halo - (s - 1) * SUBLANES, :] = _prev_chunk_rows(cur, xc_ref[s - 1])
        xc_ref[s - 1] = cur
    for s in range(1, kw):
        xr = xr + cw_ref[kw - 1 - s:kw - s, :] * xr_ref[halo - s * SUBLANES:halo - s * SUBLANES + tm, :]

    neg_c_sp = -RG_C * jax.nn.softplus(-lam_ref[...])
    rows = lax.broadcasted_iota(jnp.int32, (SUBLANES, bw), 0)
    for hd in range(heads):
        sl = slice(hd * bw, (hd + 1) * bw)
        xh = xr[:, sl]
        ax = _dot(xh.astype(BF16), wax_ref[hd])
        r = _sigmoid(ax[:, :bw] + ba_ref[:, sl])
        ig = _sigmoid(ax[:, bw:] + bx_ref[:, sl])
        log_a = r * neg_c_sp[:, sl]
        a_ref[:, sl] = jnp.exp(log_a)
        th = jnp.tanh(log_a)
        num = -2.0 * th
        mult = jnp.where(num > 0.0, num * lax.rsqrt(num * (1.0 - th)), 0.0)
        b_ref[:, sl] = mult * (ig * xh)

        e = b_ref[0:SUBLANES, sl]
        p = a_ref[0:SUBLANES, sl]
        for tl in range(1, steps):
            blk = slice(tl * SUBLANES, (tl + 1) * SUBLANES)
            a = a_ref[blk, sl]
            e = a * e + b_ref[blk, sl]
            p = a * p
            b_ref[blk, sl] = e
            a_ref[blk, sl] = p
        for k in (1, 2, 4):
            keep = rows >= k
            e = e + p * jnp.where(keep, pltpu.roll(e, k, 0), 0.0)
            p = p * jnp.where(keep, pltpu.roll(p, k, 0), 1.0)
        hin = hc_ref[:, sl]
        e = e + p * hin
        init = jnp.where(rows == 0, hin, pltpu.roll(e, 1, 0))
        hc_ref[:, sl] = _bcast_last_row(e)

        hs = (b_ref[:, sl].reshape(steps, SUBLANES, bw)
              + a_ref[:, sl].reshape(steps, SUBLANES, bw) * init[None]).reshape(tm, bw)
        y_ref[:, sl] = (hs * jax.nn.gelu(gate[:, sl])).astype(BF16)
    o_ref[...] = h + _dot(y_ref[...], wout_ref[...])


def _rg_block(h, g, w_in, conv_w, conv_b, w_a, b_a, w_x, b_x, lam, w_out, *, seq):
    t, d = h.shape
    tm = TILE_ROWS
    heads = w_a.shape[0]
    kw = conv_w.shape[0]
    wax = jnp.concatenate([w_a, w_x], axis=-1).astype(BF16)
    row = pl.BlockSpec((tm, d), lambda i: (i, 0))
    args = (g.reshape(1, d), w_in.astype(BF16), conv_w, conv_b.reshape(1, d), wax,
            b_a.reshape(1, d), b_x.reshape(1, d), lam.reshape(1, d), w_out.astype(BF16))
    kern = functools.partial(_rg_kernel, tiles_per_seq=seq // tm, heads=heads)
    return pl.pallas_call(
        kern,
        grid=(t // tm,),
        in_specs=[row] + [_full_spec(a) for a in args],
        out_specs=row,
        out_shape=jax.ShapeDtypeStruct((t, d), F32),
        scratch_shapes=[pltpu.VMEM(((kw - 1) * SUBLANES + tm, d), F32),
                        pltpu.VMEM((tm, d + ROW_PAD), F32),
                        pltpu.VMEM((tm, d + ROW_PAD), F32),
                        pltpu.VMEM((kw - 1, SUBLANES, d), F32),
                        pltpu.VMEM((SUBLANES, d), F32),
                        pltpu.VMEM((tm, d), BF16)],
        compiler_params=pltpu.CompilerParams(dimension_semantics=("arbitrary",),
                                             vmem_limit_bytes=VMEM_LIMIT_BYTES),
        name="rglru_mixer",
    )(h, *args)


def _s5_kernel(h_ref, g_ref, win_ref, bblk_ref, tab_ref, pw_ref, cblk_ref, d_ref, wglu_ref,
               wout_ref, o_ref, u_ref, st_ref, sb_ref, y_ref, hc_ref, *, tiles_per_seq,
               lane_blocks):
    tm, d = h_ref.shape
    half = bblk_ref.shape[2] // 2
    n_lc = half // LANES
    steps = tm // SUBLANES
    i = pl.program_id(0)

    @pl.when(i % tiles_per_seq == 0)
    def _():
        hc_ref[...] = jnp.zeros_like(hc_ref)

    h = h_ref[...]
    hn = _rmsnorm(h, g_ref[...]).astype(BF16)
    u_ref[...] = _dot(hn, win_ref[...])
    row0 = _is_row0((SUBLANES, LANES))
    ns = 2 * half

    def lanes(lc):
        return (slice(lc * LANES, (lc + 1) * LANES),
                slice(half + lc * LANES, half + (lc + 1) * LANES))

    def project_in(j):
        uj = u_ref[:, j * LANES:(j + 1) * LANES].astype(BF16)
        st_ref[j % 2, :, 0:ns] = _dot(uj, bblk_ref[j])

    def pair(blocks):
        return jnp.concatenate(blocks, axis=0).astype(BF16)

    project_in(0)
    for j in range(lane_blocks):
        if j + 1 < lane_blocks:
            project_in(j + 1)
        st = st_ref.at[j % 2]
        sb = sb_ref.at[(j // 2) % 2]
        off = (j % 2) * ns

        abar = [(jnp.broadcast_to(tab_ref[j, 2 * lc, 0:1, :], (SUBLANES, LANES)),
                 jnp.broadcast_to(tab_ref[j, 2 * lc + 1, 0:1, :], (SUBLANES, LANES)))
                for lc in range(n_lc)]
        ends = [None] * n_lc
        for t2 in range(0, steps, 2):
            rows2 = slice(t2 * SUBLANES, (t2 + 2) * SUBLANES)
            for lc in range(n_lc):
                lr, li = lanes(lc)
                ar, ai = abar[lc]
                blk_r, blk_i = [], []
                for tl in (t2, t2 + 1):
                    rows = slice(tl * SUBLANES, (tl + 1) * SUBLANES)
                    if tl == 0:
                        er, ei = st[rows, lr], st[rows, li]
                    else:
                        er, ei = ends[lc]
                        er, ei = (ar * er - ai * ei + st[rows, lr], ar * ei + ai * er + st[rows, li])
                    ends[lc] = (er, ei)
                    blk_r.append(er)
                    blk_i.append(ei)
                sb[rows2, off + lc * LANES:off + (lc + 1) * LANES] = pair(blk_r)
                sb[rows2, off + half + lc * LANES:off + half + (lc + 1) * LANES] = pair(blk_i)

        for lc in range(n_lc):
            lr, li = lanes(lc)
            er, ei = ends[lc]
            for s in range(3):
                k = 1 << s
                pr = pw_ref[j, 2 * s, :, lr]
                pi = pw_ref[j, 2 * s + 1, :, lr]
                sr = pltpu.roll(er, k, 0)
                si = pltpu.roll(ei, k, 0)
                er, ei = er + pr * sr - pi * si, ei + pr * si + pi * sr
            qr = pw_ref[j, 6, :, lr]
            qi = pw_ref[j, 7, :, lr]
            cr = hc_ref[j, :, lr]
            ci = hc_ref[j, :, li]
            er, ei = er + qr * cr - qi * ci, ei + qr * ci + qi * cr
            zr = jnp.where(row0, cr, pltpu.roll(er, 1, 0))
            zi = jnp.where(row0, ci, pltpu.roll(ei, 1, 0))
            hc_ref[j, :, lr] = _bcast_last_row(er)
            hc_ref[j, :, li] = _bcast_last_row(ei)

            zr2 = pair([zr, zr])
            zi2 = pair([zi, zi])
            for t2 in range(0, steps, 2):
                tr2 = pair([tab_ref[j, 2 * lc, pl.ds(tl, SUBLANES, stride=0), :] for tl in (t2, t2 + 1)])
                ti2 = pair([tab_ref[j, 2 * lc + 1, pl.ds(tl, SUBLANES, stride=0), :] for tl in (t2, t2 + 1)])
                rows2 = slice(t2 * SUBLANES, (t2 + 2) * SUBLANES)
                fr = off + lc * LANES
                fi = off + half + lc * LANES
                sb[rows2, fr:fr + LANES] = sb[rows2, fr:fr + LANES] + (tr2 * zr2 - ti2 * zi2)
                sb[rows2, fi:fi + LANES] = sb[rows2, fi:fi + LANES] + (tr2 * zi2 + ti2 * zr2)

        if j % 2 == 1:
            jp = j // 2
            y_ref[:, jp * 2 * LANES:(jp + 1) * 2 * LANES] = _dot(sb[:, 0:2 * ns], cblk_ref[jp])

    y = y_ref[...] + d_ref[...] * u_ref[...]
    gl = _dot(jax.nn.gelu(y).astype(BF16), wglu_ref[...])
    out = (gl[:, :d] * _sigmoid(gl[:, d:])).astype(BF16)
    o_ref[...] = h + _dot(out, wout_ref[...])


def _cmul(x, y):
    return x[0] * y[0] - x[1] * y[1], x[0] * y[1] + x[1] * y[0]


def _cpowers(a, n):
    pr, pi = a
    while pr.shape[1] < n:
        tr, ti = _cmul((pr, pi), (pr[:, -1:], pi[:, -1:]))
        pr = jnp.concatenate([pr, tr], axis=1)
        pi = jnp.concatenate([pi, ti], axis=1)
    return pr[:, :n], pi[:, :n]


def _s5_params(a_re, a_im, log_dt, b_re, b_im, c_re, c_im, chunk):
    g, p = a_re.shape
    gc = b_re.shape[-1]
    gpb = LANES // gc
    nb = g // gpb
    dt = jnp.exp(log_dt)[:, None]
    mag = jnp.exp(a_re * dt)
    abr = mag * jnp.cos(a_im * dt)
    abi = mag * jnp.sin(a_im * dt)
    ur, ui = abr - 1.0, abi
    den = a_re * a_re + a_im * a_im
    wr = (ur * a_re + ui * a_im) / den
    wi = (ui * a_re - ur * a_im) / den
    bbr = wr[..., None] * b_re - wi[..., None] * b_im
    bbi = wr[..., None] * b_im + wi[..., None] * b_re
    eye = jnp.eye(gpb, dtype=F32)
    bb = jnp.stack([bbr, bbi]).reshape(2, nb, gpb, p, gc)
    bblk = jnp.einsum('ab,zjapc->jaczbp', eye, bb).reshape(nb, gpb * gc, 2 * gpb * p)
    cc = jnp.stack([c_re, -c_im]).reshape(2, nb, gpb, gc, p)
    cblk = jnp.einsum('ab,zjacp->jzapbc', eye, cc).reshape(nb, 2 * gpb * p, gpb * gc)
    a1 = (abr.reshape(nb, 1, gpb * p), abi.reshape(nb, 1, gpb * p))
    apow = _cpowers(a1, chunk)
    tab = jnp.stack(apow, axis=1)
    tab = tab.reshape(nb, 2, chunk, gpb * p // LANES, LANES).transpose(0, 3, 1, 2, 4)
    tab = tab.reshape(nb, 2 * gpb * p // LANES, chunk, LANES)
    tab = jnp.pad(tab, ((0, 0), (0, 0), (0, SUBLANES), (0, 0)))
    wpow = _cpowers((apow[0][:, -1:], apow[1][:, -1:]), SUBLANES)
    rows = jnp.arange(SUBLANES)[None, :, None]
    planes = []
    for k in (1, 2, 4):
        for part in range(2):
            planes.append(jnp.where(rows >= k, wpow[part][:, k - 1:k], 0.0))
    for part in range(2):
        planes.append(wpow[part])
    pw = jnp.stack(planes, axis=1)
    zeros = jnp.zeros_like(cblk[0::2])
    cblk = jnp.concatenate([jnp.concatenate([cblk[0::2], zeros], axis=2),
                            jnp.concatenate([zeros, cblk[1::2]], axis=2)], axis=1)
    return bblk.astype(BF16), tab, pw, cblk.astype(BF16)


def _s5_block(h, g, w_in, params, dvec, w_glu, w_out, *, seq):
    t, d = h.shape
    tm = TILE_ROWS
    bblk, tab, pw, cblk = params
    nb = bblk.shape[0]
    ns = bblk.shape[2]
    row = pl.BlockSpec((tm, d), lambda i: (i, 0))
    args = (g.reshape(1, d), w_in.astype(BF16), bblk, tab, pw, cblk, dvec.reshape(1, d),
            w_glu.astype(BF16), w_out.astype(BF16))
    kern = functools.partial(_s5_kernel, tiles_per_seq=seq // tm, lane_blocks=nb)
    return pl.pallas_call(
        kern,
        grid=(t // tm,),
        in_specs=[row] + [_full_spec(a) for a in args],
        out_specs=row,
        out_shape=jax.ShapeDtypeStruct((t, d), F32),
        scratch_shapes=[pltpu.VMEM((tm, d), F32),
                        pltpu.VMEM((2, tm, ns + ROW_PAD), F32),
                        pltpu.VMEM((2, tm, 2 * ns + ROW_PAD), BF16),
                        pltpu.VMEM((tm, d), F32),
                        pltpu.VMEM((nb, SUBLANES, ns), F32)],
        compiler_params=pltpu.CompilerParams(dimension_semantics=("arbitrary",),
                                             vmem_limit_bytes=VMEM_LIMIT_BYTES),
        name="s5_mixer",
    )(h, *args)


def kernel(x, norm_mix_g, norm_ffn_g, norm_final_g, rg_w_in, rg_conv_w, rg_conv_b, rg_w_a, rg_b_a, rg_w_x, rg_b_x, rg_lambda, rg_w_out, s5_w_in, s5_a_re, s5_a_im, s5_log_dt, s5_b_re, s5_b_im, s5_c_re, s5_c_im, s5_d, s5_w_glu, s5_w_out, ffn_w_up, ffn_conv_w, ffn_conv_b, ffn_w_down):
    bsz, seq, d = x.shape
    depth = norm_mix_g.shape[0]
    assert seq % TILE_ROWS == 0
    h = _to_tile_order(x.reshape(bsz * seq, d), TILE_ROWS)
    s5_params = jax.vmap(functools.partial(_s5_params, chunk=TILE_ROWS // SUBLANES))(
        s5_a_re, s5_a_im, s5_log_dt, s5_b_re, s5_b_im, s5_c_re, s5_c_im)
    for i in range(depth):
        j = i // 2
        if i % 2 == 0:
            h = _rg_block(h, norm_mix_g[i], rg_w_in[j], rg_conv_w[j], rg_conv_b[j], rg_w_a[j],
                          rg_b_a[j].reshape(-1), rg_w_x[j], rg_b_x[j].reshape(-1), rg_lambda[j],
                          rg_w_out[j], seq=seq)
        else:
            h = _s5_block(h, norm_mix_g[i], s5_w_in[j], [p[j] for p in s5_params], s5_d[j],
                          s5_w_glu[j], s5_w_out[j], seq=seq)
        h = _ffn_block(h, norm_ffn_g[i], ffn_w_up[i], ffn_conv_w[i], ffn_conv_b[i],
                       ffn_w_down[i], norm_final_g, seq=seq, final_norm=(i == depth - 1))
    return h.reshape(bsz, seq, d)
```

```python
import functools

import jax
import jax.numpy as jnp
from jax import lax
from jax.experimental import pallas as pl
from jax.experimental.pallas import tpu as pltpu

NORM_EPS = 1e-6
RG_C = 8.0
SUBLANES = 8
LANES = 128
VMEM_LIMIT_BYTES = 56 * 1024 * 1024
TILE_ROWS = 512
FFN_CHUNK = 512
ROW_PAD = LANES

F32 = jnp.float32
BF16 = jnp.bfloat16


def _rmsnorm(x, g):
    var = jnp.mean(x * x, axis=-1, keepdims=True)
    return x * lax.rsqrt(var + NORM_EPS) * g


def _sigmoid(x):
    return 0.5 * jnp.tanh(0.5 * x) + 0.5


def _dot(a, b):
    return jnp.dot(a, b, preferred_element_type=F32)


def _is_row0(shape):
    return lax.broadcasted_iota(jnp.int32, shape, 0) == 0


def _prev_chunk_rows(cur, prev_tile):
    return jnp.where(_is_row0(cur.shape), pltpu.roll(prev_tile, 1, 0), pltpu.roll(cur, 1, 0))


def _bcast_last_row(x):
    return jnp.broadcast_to(x[SUBLANES - 1:SUBLANES, :], x.shape)


def _full_spec(a):
    nd = a.ndim
    return pl.BlockSpec(a.shape, lambda i: (0,) * nd, pipeline_mode=pl.Buffered(1))


def _ffn_kernel(h_ref, g_ref, wa_ref, wb_ref, cwa_ref, cwb_ref, cba_ref, cbb_ref,
                wd_ref, gf_ref, o_ref, xa_ref, xb_ref, ca_ref, cb_ref, act_ref, un_ref,
                *, tiles_per_seq, n_chunks, final_norm):
    tm = h_ref.shape[0]
    kw = cwa_ref.shape[1]
    halo = (kw - 1) * SUBLANES
    i = pl.program_id(0)

    @pl.when(i % tiles_per_seq == 0)
    def _():
        ca_ref[...] = jnp.zeros_like(ca_ref)
        cb_ref[...] = jnp.zeros_like(cb_ref)

    h = h_ref[...]
    hn = _rmsnorm(h, g_ref[...]).astype(BF16)

    def conv(x_ref, carry_ref, c, up, cw, cb):
        x_ref[c, halo:halo + tm, :] = up
        out = cb + cw[kw - 1:kw, :] * up
        for s in range(1, kw):
            cur = up[tm - s * SUBLANES:tm - (s - 1) * SUBLANES, :]
            x_ref[c, halo - s * SUBLANES:halo - (s - 1) * SUBLANES, :] = (
                _prev_chunk_rows(cur, carry_ref[c, s - 1]))
            carry_ref[c, s - 1] = cur
        for s in range(1, kw):
            out = out + cw[kw - 1 - s:kw - s, :] * x_ref[c, halo - s * SUBLANES:halo - s * SUBLANES + tm, :]
        return out

    fc = wa_ref.shape[2]
    for c in range(n_chunks):
        ua = _dot(hn, wa_ref[c])
        ub = _dot(hn, wb_ref[c])
        va = conv(xa_ref, ca_ref, c, ua, cwa_ref[c], cba_ref[c])
        vb = conv(xb_ref, cb_ref, c, ub, cwb_ref[c], cbb_ref[c])
        act_ref[:, c * fc:(c + 1) * fc] = (jax.nn.gelu(va) * vb).astype(BF16)
    acc = h + _dot(act_ref[...], wd_ref[...])
    if final_norm:
        acc = _rmsnorm(acc, gf_ref[...])
        d = acc.shape[1]
        steps = tm // SUBLANES
        for c in range(d // LANES):
            un_ref[c] = acc[:, c * LANES:(c + 1) * LANES]
        for q in range(SUBLANES):
            for c in range(d // LANES):
                o_ref[q * steps:(q + 1) * steps, c * LANES:(c + 1) * LANES] = (
                    un_ref[c, pl.ds(q, steps, stride=SUBLANES), :])
    else:
        o_ref[...] = acc


def _ffn_block(h, g, w_up, conv_w, conv_b, w_down, g_final, *, seq, final_norm):
    t, d = h.shape
    tm, fc = TILE_ROWS, FFN_CHUNK
    dff = w_down.shape[0]
    nc = dff // fc
    kw = conv_w.shape[0]
    wa = w_up[:, :dff].reshape(d, nc, fc).transpose(1, 0, 2).astype(BF16)
    wb = w_up[:, dff:].reshape(d, nc, fc).transpose(1, 0, 2).astype(BF16)
    cwa = conv_w[:, :dff].reshape(kw, nc, fc).transpose(1, 0, 2)
    cwb = conv_w[:, dff:].reshape(kw, nc, fc).transpose(1, 0, 2)
    cba = conv_b[:dff].reshape(nc, 1, fc)
    cbb = conv_b[dff:].reshape(nc, 1, fc)
    wd = w_down.astype(BF16)

    row = pl.BlockSpec((tm, d), lambda i: (i, 0))
    args = (g.reshape(1, d), wa, wb, cwa, cwb, cba, cbb, wd, g_final.reshape(1, d))
    kern = functools.partial(_ffn_kernel, tiles_per_seq=seq // tm, n_chunks=nc,
                             final_norm=final_norm)
    halo = (kw - 1) * SUBLANES
    return pl.pallas_call(
        kern,
        grid=(t // tm,),
        in_specs=[row] + [_full_spec(a) for a in args],
        out_specs=row,
        out_shape=jax.ShapeDtypeStruct((t, d), F32),
        scratch_shapes=[pltpu.VMEM((nc, halo + tm, fc), F32),
                        pltpu.VMEM((nc, halo + tm, fc), F32),
                        pltpu.VMEM((nc, kw - 1, SUBLANES, fc), F32),
                        pltpu.VMEM((nc, kw - 1, SUBLANES, fc), F32),
                        pltpu.VMEM((tm, dff), BF16),
                        pltpu.VMEM((d // LANES, tm, LANES), F32)],
        compiler_params=pltpu.CompilerParams(dimension_semantics=("arbitrary",),
                                             vmem_limit_bytes=VMEM_LIMIT_BYTES),
        name="conv_ffn",
    )(h, *args)


def _rg_kernel(h_ref, g_ref, win_ref, cw_ref, cb_ref, wax_ref, ba_ref, bx_ref, lam_ref,
               wout_ref, o_ref, xr_ref, a_ref, b_ref, xc_ref, hc_ref, y_ref, hp_ref, *,
               tiles_per_seq, heads, token_order_in):
    tm, d = h_ref.shape
    bw = d // heads
    kw = cw_ref.shape[0]
    halo = (kw - 1) * SUBLANES
    steps = tm // SUBLANES
    i = pl.program_id(0)

    @pl.when(i % tiles_per_seq == 0)
    def _():
        xc_ref[...] = jnp.zeros_like(xc_ref)
        hc_ref[...] = jnp.zeros_like(hc_ref)

    if token_order_in:
        for q in range(SUBLANES):
            for c in range(d // LANES):
                hp_ref[c, pl.ds(q, steps, stride=SUBLANES), :] = (
                    h_ref[q * steps:(q + 1) * steps, c * LANES:(c + 1) * LANES])
        h = jnp.concatenate([hp_ref[c] for c in range(d // LANES)], axis=1)
    else:
        h = h_ref[...]
    hn = _rmsnorm(h, g_ref[...]).astype(BF16)
    xg = _dot(hn, win_ref[...])
    gate = xg[:, d:]
    xp = xg[:, :d]
    xr_ref[halo:halo + tm, :] = xp
    xr = cb_ref[...] + cw_ref[kw - 1:kw, :] * xp
    for s in range(1, kw):
        cur = xp[tm - s * SUBLANES:tm - (s - 1) * SUBLANES, :]
        xr_ref[halo - s * SUBLANES:halo - (s - 1) * SUBLANES, :] = _prev_chunk_rows(cur, xc_ref[s - 1])
        xc_ref[s - 1] = cur
    for s in range(1, kw):
        xr = xr + cw_ref[kw - 1 - s:kw - s, :] * xr_ref[halo - s * SUBLANES:halo - s * SUBLANES + tm, :]

    neg_c_sp = -RG_C * jax.nn.softplus(-lam_ref[...])
    rows = lax.broadcasted_iota(jnp.int32, (SUBLANES, bw), 0)
    for hd in range(heads):
        sl = slice(hd * bw, (hd + 1) * bw)
        xh = xr[:, sl]
        ax = _dot(xh.astype(BF16), wax_ref[hd])
        r = _sigmoid(ax[:, :bw] + ba_ref[:, sl])
        ig = _sigmoid(ax[:, bw:] + bx_ref[:, sl])
        log_a = r * neg_c_sp[:, sl]
        a_ref[:, sl] = jnp.exp(log_a)
        th = jnp.tanh(log_a)
        num = -2.0 * th
        mult = jnp.where(num > 0.0, num * lax.rsqrt(num * (1.0 - th)), 0.0)
        b_ref[:, sl] = mult * (ig * xh)

        e = b_ref[0:SUBLANES, sl]
        p = a_ref[0:SUBLANES, sl]
        for tl in range(1, steps):
            blk = slice(tl * SUBLANES, (tl + 1) * SUBLANES)
            a = a_ref[blk, sl]
            e = a * e + b_ref[blk, sl]
            p = a * p
            b_ref[blk, sl] = e
            a_ref[blk, sl] = p
        for k in (1, 2, 4):
            keep = rows >= k
            e = e + p * jnp.where(keep, pltpu.roll(e, k, 0), 0.0)
            p = p * jnp.where(keep, pltpu.roll(p, k, 0), 1.0)
        hin = hc_ref[:, sl]
        e = e + p * hin
        init = jnp.where(rows == 0, hin, pltpu.roll(e, 1, 0))
        hc_ref[:, sl] = _bcast_last_row(e)

        hs = (b_ref[:, sl].reshape(steps, SUBLANES, bw)
              + a_ref[:, sl].reshape(steps, SUBLANES, bw) * init[None]).reshape(tm, bw)
        y_ref[:, sl] = (hs * jax.nn.gelu(gate[:, sl])).astype(BF16)
    o_ref[...] = h + _dot(y_ref[...], wout_ref[...])


def _rg_block(h, g, w_in, conv_w, conv_b, w_a, b_a, w_x, b_x, lam, w_out, *, seq, token_order_in):
    t, d = h.shape
    tm = TILE_ROWS
    heads = w_a.shape[0]
    kw = conv_w.shape[0]
    wax = jnp.concatenate([w_a, w_x], axis=-1).astype(BF16)
    row = pl.BlockSpec((tm, d), lambda i: (i, 0))
    args = (g.reshape(1, d), w_in.astype(BF16), conv_w, conv_b.reshape(1, d), wax,
            b_a.reshape(1, d), b_x.reshape(1, d), lam.reshape(1, d), w_out.astype(BF16))
    kern = functools.partial(_rg_kernel, tiles_per_seq=seq // tm, heads=heads,
                             token_order_in=token_order_in)
    return pl.pallas_call(
        kern,
        grid=(t // tm,),
        in_specs=[row] + [_full_spec(a) for a in args],
        out_specs=row,
        out_shape=jax.ShapeDtypeStruct((t, d), F32),
        scratch_shapes=[pltpu.VMEM(((kw - 1) * SUBLANES + tm, d), F32),
                        pltpu.VMEM((tm, d + ROW_PAD), F32),
                        pltpu.VMEM((tm, d + ROW_PAD), F32),
                        pltpu.VMEM((kw - 1, SUBLANES, d), F32),
                        pltpu.VMEM((SUBLANES, d), F32),
                        pltpu.VMEM((tm, d), BF16),
                        pltpu.VMEM((d // LANES, tm, LANES), F32)],
        compiler_params=pltpu.CompilerParams(dimension_semantics=("arbitrary",),
                                             vmem_limit_bytes=VMEM_LIMIT_BYTES),
        name="rglru_mixer",
    )(h, *args)


def _s5_kernel(h_ref, g_ref, win_ref, bblk_ref, tab_ref, pw_ref, cblk_ref, d_ref, wglu_ref,
               wout_ref, o_ref, u_ref, st_ref, sb_ref, y_ref, hc_ref, *, tiles_per_seq,
               lane_blocks):
    tm, d = h_ref.shape
    half = bblk_ref.shape[2] // 2
    n_lc = half // LANES
    steps = tm // SUBLANES
    i = pl.program_id(0)

    @pl.when(i % tiles_per_seq == 0)
    def _():
        hc_ref[...] = jnp.zeros_like(hc_ref)

    h = h_ref[...]
    hn = _rmsnorm(h, g_ref[...]).astype(BF16)
    u_ref[...] = _dot(hn, win_ref[...])
    row0 = _is_row0((SUBLANES, LANES))
    ns = 2 * half

    def lanes(lc):
        return (slice(lc * LANES, (lc + 1) * LANES),
                slice(half + lc * LANES, half + (lc + 1) * LANES))

    def project_in(j):
        uj = u_ref[:, j * LANES:(j + 1) * LANES].astype(BF16)
        st_ref[j % 2, :, 0:ns] = _dot(uj, bblk_ref[j])

    def pair(blocks):
        return jnp.concatenate(blocks, axis=0).astype(BF16)

    project_in(0)
    for j in range(lane_blocks):
        if j + 1 < lane_blocks:
            project_in(j + 1)
        st = st_ref.at[j % 2]
        sb = sb_ref.at[(j // 2) % 2]
        off = (j % 2) * ns

        abar = [(jnp.broadcast_to(tab_ref[j, 2 * lc, 0:1, :], (SUBLANES, LANES)),
                 jnp.broadcast_to(tab_ref[j, 2 * lc + 1, 0:1, :], (SUBLANES, LANES)))
                for lc in range(n_lc)]
        ends = [None] * n_lc
        for t2 in range(0, steps, 2):
            rows2 = slice(t2 * SUBLANES, (t2 + 2) * SUBLANES)
            for lc in range(n_lc):
                lr, li = lanes(lc)
                ar, ai = abar[lc]
                blk_r, blk_i = [], []
                for tl in (t2, t2 + 1):
                    rows = slice(tl * SUBLANES, (tl + 1) * SUBLANES)
                    if tl == 0:
                        er, ei = st[rows, lr], st[rows, li]
                    else:
                        er, ei = ends[lc]
                        er, ei = (ar * er - ai * ei + st[rows, lr], ar * ei + ai * er + st[rows, li])
                    ends[lc] = (er, ei)
                    blk_r.append(er)
                    blk_i.append(ei)
                sb[rows2, off + lc * LANES:off + (lc + 1) * LANES] = pair(blk_r)
                sb[rows2, off + half + lc * LANES:off + half + (lc + 1) * LANES] = pair(blk_i)

        for lc in range(n_lc):
            lr, li = lanes(lc)
            er, ei = ends[lc]
            for s in range(3):
                k = 1 << s
                pr = pw_ref[j, 2 * s, :, lr]
                pi = pw_ref[j, 2 * s + 1, :, lr]
                sr = pltpu.roll(er, k, 0)
                si = pltpu.roll(ei, k, 0)
                er, ei = er + pr * sr - pi * si, ei + pr * si + pi * sr
            qr = pw_ref[j, 6, :, lr]
            qi = pw_ref[j, 7, :, lr]
            cr = hc_ref[j, :, lr]
            ci = hc_ref[j, :, li]
            er, ei = er + qr * cr - qi * ci, ei + qr * ci + qi * cr
            zr = jnp.where(row0, cr, pltpu.roll(er, 1, 0))
            zi = jnp.where(row0, ci, pltpu.roll(ei, 1, 0))
            hc_ref[j, :, lr] = _bcast_last_row(er)
            hc_ref[j, :, li] = _bcast_last_row(ei)

            zr2 = pair([zr, zr])
            zi2 = pair([zi, zi])
            for t2 in range(0, steps, 2):
                tr2 = pair([tab_ref[j, 2 * lc, pl.ds(tl, SUBLANES, stride=0), :] for tl in (t2, t2 + 1)])
                ti2 = pair([tab_ref[j, 2 * lc + 1, pl.ds(tl, SUBLANES, stride=0), :] for tl in (t2, t2 + 1)])
                rows2 = slice(t2 * SUBLANES, (t2 + 2) * SUBLANES)
                fr = off + lc * LANES
                fi = off + half + lc * LANES
                sb[rows2, fr:fr + LANES] = sb[rows2, fr:fr + LANES] + (tr2 * zr2 - ti2 * zi2)
                sb[rows2, fi:fi + LANES] = sb[rows2, fi:fi + LANES] + (tr2 * zi2 + ti2 * zr2)

        if j % 2 == 1:
            jp = j // 2
            y_ref[:, jp * 2 * LANES:(jp + 1) * 2 * LANES] = _dot(sb[:, 0:2 * ns], cblk_ref[jp])

    y = y_ref[...] + d_ref[...] * u_ref[...]
    gl = _dot(jax.nn.gelu(y).astype(BF16), wglu_ref[...])
    out = (gl[:, :d] * _sigmoid(gl[:, d:])).astype(BF16)
    o_ref[...] = h + _dot(out, wout_ref[...])


def _cmul(x, y):
    return x[0] * y[0] - x[1] * y[1], x[0] * y[1] + x[1] * y[0]


def _cpowers(a, n):
    pr, pi = a
    while pr.shape[1] < n:
        tr, ti = _cmul((pr, pi), (pr[:, -1:], pi[:, -1:]))
        pr = jnp.concatenate([pr, tr], axis=1)
        pi = jnp.concatenate([pi, ti], axis=1)
    return pr[:, :n], pi[:, :n]


def _s5_params(a_re, a_im, log_dt, b_re, b_im, c_re, c_im, chunk):
    g, p = a_re.shape
    gc = b_re.shape[-1]
    gpb = LANES // gc
    nb = g // gpb
    dt = jnp.exp(log_dt)[:, None]
    mag = jnp.exp(a_re * dt)
    abr = mag * jnp.cos(a_im * dt)
    abi = mag * jnp.sin(a_im * dt)
    ur, ui = abr - 1.0, abi
    den = a_re * a_re + a_im * a_im
    wr = (ur * a_re + ui * a_im) / den
    wi = (ui * a_re - ur * a_im) / den
    bbr = wr[..., None] * b_re - wi[..., None] * b_im
    bbi = wr[..., None] * b_im + wi[..., None] * b_re
    eye = jnp.eye(gpb, dtype=F32)
    bb = jnp.stack([bbr, bbi]).reshape(2, nb, gpb, p, gc)
    bblk = jnp.einsum('ab,zjapc->jaczbp', eye, bb).reshape(nb, gpb * gc, 2 * gpb * p)
    cc = jnp.stack([c_re, -c_im]).reshape(2, nb, gpb, gc, p)
    cblk = jnp.einsum('ab,zjacp->jzapbc', eye, cc).reshape(nb, 2 * gpb * p, gpb * gc)
    a1 = (abr.reshape(nb, 1, gpb * p), abi.reshape(nb, 1, gpb * p))
    apow = _cpowers(a1, chunk)
    tab = jnp.stack(apow, axis=1)
    tab = tab.reshape(nb, 2, chunk, gpb * p // LANES, LANES).transpose(0, 3, 1, 2, 4)
    tab = tab.reshape(nb, 2 * gpb * p // LANES, chunk, LANES)
    tab = jnp.pad(tab, ((0, 0), (0, 0), (0, SUBLANES), (0, 0)))
    wpow = _cpowers((apow[0][:, -1:], apow[1][:, -1:]), SUBLANES)
    rows = jnp.arange(SUBLANES)[None, :, None]
    planes = []
    for k in (1, 2, 4):
        for part in range(2):
            planes.append(jnp.where(rows >= k, wpow[part][:, k - 1:k], 0.0))
    for part in range(2):
        planes.append(wpow[part])
    pw = jnp.stack(planes, axis=1)
    zeros = jnp.zeros_like(cblk[0::2])
    cblk = jnp.concatenate([jnp.concatenate([cblk[0::2], zeros], axis=2),
                            jnp.concatenate([zeros, cblk[1::2]], axis=2)], axis=1)
    return bblk.astype(BF16), tab, pw, cblk.astype(BF16)


def _s5_block(h, g, w_in, params, dvec, w_glu, w_out, *, seq):
    t, d = h.shape
    tm = TILE_ROWS
    bblk, tab, pw, cblk = params
    nb = bblk.shape[0]
    ns = bblk.shape[2]
    row = pl.BlockSpec((tm, d), lambda i: (i, 0))
    args = (g.reshape(1, d), w_in.astype(BF16), bblk, tab, pw, cblk, dvec.reshape(1, d),
            w_glu.astype(BF16), w_out.astype(BF16))
    kern = functools.partial(_s5_kernel, tiles_per_seq=seq // tm, lane_blocks=nb)
    return pl.pallas_call(
        kern,
        grid=(t // tm,),
        in_specs=[row] + [_full_spec(a) for a in args],
        out_specs=row,
        out_shape=jax.ShapeDtypeStruct((t, d), F32),
        scratch_shapes=[pltpu.VMEM((tm, d), F32),
                        pltpu.VMEM((2, tm, ns + ROW_PAD), F32),
                        pltpu.VMEM((2, tm, 2 * ns + ROW_PAD), BF16),
                        pltpu.VMEM((tm, d), F32),
                        pltpu.VMEM((nb, SUBLANES, ns), F32)],
        compiler_params=pltpu.CompilerParams(dimension_semantics=("arbitrary",),
                                             vmem_limit_bytes=VMEM_LIMIT_BYTES),
        name="s5_mixer",
    )(h, *args)


def kernel(x, norm_mix_g, norm_ffn_g, norm_final_g, rg_w_in, rg_conv_w, rg_conv_b, rg_w_a, rg_b_a, rg_w_x, rg_b_x, rg_lambda, rg_w_out, s5_w_in, s5_a_re, s5_a_im, s5_log_dt, s5_b_re, s5_b_im, s5_c_re, s5_c_im, s5_d, s5_w_glu, s5_w_out, ffn_w_up, ffn_conv_w, ffn_conv_b, ffn_w_down):
    bsz, seq, d = x.shape
    depth = norm_mix_g.shape[0]
    assert seq % TILE_ROWS == 0
    h = x.reshape(bsz * seq, d)
    s5_params = jax.vmap(functools.partial(_s5_params, chunk=TILE_ROWS // SUBLANES))(
        s5_a_re, s5_a_im, s5_log_dt, s5_b_re, s5_b_im, s5_c_re, s5_c_im)
    for i in range(depth):
        j = i // 2
        if i % 2 == 0:
            h = _rg_block(h, norm_mix_g[i], rg_w_in[j], rg_conv_w[j], rg_conv_b[j], rg_w_a[j],
                          rg_b_a[j].reshape(-1), rg_w_x[j], rg_b_x[j].reshape(-1), rg_lambda[j],
                          rg_w_out[j], seq=seq, token_order_in=(i == 0))
        else:
            h = _s5_block(h, norm_mix_g[i], s5_w_in[j], [p[j] for p in s5_params], s5_d[j],
                          s5_w_glu[j], s5_w_out[j], seq=seq)
        h = _ffn_block(h, norm_ffn_g[i], ffn_w_up[i], ffn_conv_w[i], ffn_conv_b[i],
                       ffn_w_down[i], norm_final_g, seq=seq, final_norm=(i == depth - 1))
    return h.reshape(bsz, seq, d)
```
